```python
import math
import jax, jax.numpy as jnp
from jax import lax
import numpy as np

D_MODEL = 2048
BATCH = 2
SEQ = 8192
DEPTH = 2

D_MIX = D_MODEL
D_LRU = D_MIX // 2
LRU_BLOCKS = 8
LRU_BLOCK_W = D_LRU // LRU_BLOCKS
LRU_C = 8.0
CONV_W = 4
D_DN = D_MIX - D_LRU
DN_HEAD_DIM = 128
DN_HEADS = D_DN // DN_HEAD_DIM
DN_CHUNK = 64
D_IN = 2 * D_LRU + 4 * D_DN + 2 * DN_HEADS
D_FF = 3 * D_MODEL
N_EXPERTS = 8
TOP_K = 2
D_FF_EXPERT = 3 * D_MODEL // 2
MOE_BLOCK = 128
N_DENSE = (DEPTH + 1) // 2
N_MOE = DEPTH // 2
EPS = 1e-6

kernel_name = 'hybrid_rglru_gdn_moe'


def rms_norm(x, gain):
    xf = x.astype(jnp.float32)
    y = xf * lax.rsqrt(jnp.mean(xf * xf, axis=-1, keepdims=True) + EPS)
    return (y * gain.astype(jnp.float32)).astype(x.dtype)


def causal_depthwise_conv(x, w):
    k_w = w.shape[0]
    s = x.shape[1]
    xp = jnp.pad(x, ((0, 0), (k_w - 1, 0), (0, 0)))
    y = xp[:, 0:s] * w[0]
    for k in range(1, k_w):
        y = y + xp[:, k:k + s] * w[k]
    return y


def l2_normalize(x):
    return x * lax.rsqrt(jnp.sum(x * x, axis=-1, keepdims=True) + EPS)


def lru_combine(c1, c2):
    a1, b1 = c1
    a2, b2 = c2
    return a1 * a2, a2 * b1 + b2


def rg_lru_group(x_in, gate_in, conv_w, conv_b, w_r, b_r, w_i, b_i, lam, out_norm):
    bsz, s, _ = x_in.shape
    xc = (causal_depthwise_conv(x_in, conv_w) + conv_b).astype(jnp.float32)
    xb = xc.reshape(bsz, s, LRU_BLOCKS, LRU_BLOCK_W)
    r = jax.nn.sigmoid(jnp.einsum('bsnc,ncd->bsnd', xb, w_r.astype(jnp.float32)) + b_r.astype(jnp.float32))
    i = jax.nn.sigmoid(jnp.einsum('bsnc,ncd->bsnd', xb, w_i.astype(jnp.float32)) + b_i.astype(jnp.float32))
    r = r.reshape(bsz, s, D_LRU)
    i = i.reshape(bsz, s, D_LRU)
    log_a = -LRU_C * r * jax.nn.softplus(-lam.astype(jnp.float32))
    a = jnp.exp(log_a)
    mult = jnp.sqrt(-jnp.expm1(2.0 * log_a))
    b = mult * (i * xc)
    _, h = lax.associative_scan(lru_combine, (a, b), axis=1)
    y = h * jax.nn.gelu(gate_in.astype(jnp.float32), approximate=True)
    return rms_norm(y, out_norm)


def chunk_gated_delta_rule(q, k, v, g, beta):
    bsz, nh, s, dk = q.shape
    dv = v.shape[-1]
    c = DN_CHUNK
    n = s // c
    q = q.reshape(bsz, nh, n, c, dk)
    k = k.reshape(bsz, nh, n, c, dk)
    v = v.reshape(bsz, nh, n, c, dv)
    g = jnp.cumsum(g.reshape(bsz, nh, n, c), axis=-1)
    beta = beta.reshape(bsz, nh, n, c)
    causal = jnp.tril(jnp.ones((c, c), dtype=bool))
    strict = jnp.tril(jnp.ones((c, c), dtype=bool), -1)
    diff = g[..., :, None] - g[..., None, :]
    decay = jnp.where(causal, jnp.exp(jnp.where(causal, diff, 0.0)), 0.0)
    k_beta = k * beta[..., None]
    v_beta = v * beta[..., None]
    lower = jnp.where(strict, jnp.einsum('bhncd,bhnmd->bhncm', k_beta, k) * decay, 0.0)
    a_mat = lower + jnp.eye(c, dtype=jnp.float32)
    rhs = jnp.concatenate([v_beta, k_beta * jnp.exp(g)[..., None]], axis=-1)
    sol = lax.linalg.triangular_solve(a_mat, rhs, left_side=True, lower=True, unit_diagonal=True)
    u = sol[..., :dv]
    w = sol[..., dv:]
    qk = jnp.where(causal, jnp.einsum('bhncd,bhnmd->bhncm', q, k) * decay, 0.0)
    q_dec = q * jnp.exp(g)[..., None]
    k_dec = k * jnp.exp(g[..., -1:] - g)[..., None]
    g_last = jnp.exp(g[..., -1])

    def step(state, inp):
        q_i, k_i, u_i, w_i, qk_i, gl_i = inp
        v_new = u_i - jnp.einsum('bhck,bhkv->bhcv', w_i, state)
        o = jnp.einsum('bhck,bhkv->bhcv', q_i, state) + jnp.einsum('bhcm,bhmv->bhcv', qk_i, v_new)
        state = state * gl_i[..., None, None] + jnp.einsum('bhck,bhcv->bhkv', k_i, v_new)
        return state, o

    xs = tuple(jnp.moveaxis(t, 2, 0) for t in (q_dec, k_dec, u, w, qk, g_last))
    state0 = jnp.zeros((bsz, nh, dk, dv), jnp.float32)
    _, o = lax.scan(step, state0, xs)
    return jnp.moveaxis(o, 0, 2).reshape(bsz, nh, s, dv)


def gated_deltanet_group(q_in, k_in, v_in, gate_in, beta_in, alpha_in, conv_w, a_log, dt_bias, out_norm):
    bsz, s, _ = q_in.shape
    qkv = jnp.concatenate([q_in, k_in, v_in], axis=-1)
    qkv = jax.nn.silu(causal_depthwise_conv(qkv, conv_w)).astype(jnp.float32)
    q, k, v = jnp.split(qkv, 3, axis=-1)
    to_heads = lambda t: t.reshape(bsz, s, DN_HEADS, DN_HEAD_DIM).transpose(0, 2, 1, 3)
    q = l2_normalize(to_heads(q)) * (DN_HEAD_DIM ** -0.5)
    k = l2_normalize(to_heads(k))
    v = to_heads(v)
    beta = jax.nn.sigmoid(beta_in.astype(jnp.float32)).transpose(0, 2, 1)
    g = (-jnp.exp(a_log.astype(jnp.float32))
         * jax.nn.softplus(alpha_in.astype(jnp.float32) + dt_bias.astype(jnp.float32))).transpose(0, 2, 1)
    o = chunk_gated_delta_rule(q, k, v, g, beta).transpose(0, 2, 1, 3)
    gate = gate_in.astype(jnp.float32).reshape(bsz, s, DN_HEADS, DN_HEAD_DIM)
    o = rms_norm(o, out_norm) * jax.nn.silu(gate)
    return o.reshape(bsz, s, D_DN)


def swiglu(h, w_gate, w_up, w_down):
    return (jax.nn.silu(h @ w_gate) * (h @ w_up)) @ w_down


def moe_swiglu(h, router, w_gate, w_up, w_down):
    bsz, s, d = h.shape
    t = bsz * s
    xt = h.reshape(t, d)
    logits = (xt @ router).astype(jnp.float32)
    top_logits, top_e = lax.top_k(logits, TOP_K)
    top_w = jax.nn.softmax(top_logits, axis=-1)
    n_assign = t * TOP_K
    flat_e = top_e.reshape(n_assign).astype(jnp.int32)
    flat_tok = jnp.repeat(jnp.arange(t, dtype=jnp.int32), TOP_K)
    flat_w = top_w.reshape(n_assign)
    order = jnp.argsort(flat_e)
    sorted_e = flat_e[order]
    sorted_tok = flat_tok[order]
    sorted_w = flat_w[order]
    counts = jnp.zeros((N_EXPERTS,), jnp.int32).at[flat_e].add(1)
    padded = (counts + MOE_BLOCK - 1) // MOE_BLOCK * MOE_BLOCK
    pad_end = jnp.cumsum(padded)
    pad_start = pad_end - padded
    start = jnp.cumsum(counts) - counts
    rank = jnp.arange(n_assign, dtype=jnp.int32) - start[sorted_e]
    dest = pad_start[sorted_e] + rank
    n_slots = (n_assign + MOE_BLOCK - 1) // MOE_BLOCK * MOE_BLOCK + N_EXPERTS * MOE_BLOCK
    n_blocks = n_slots // MOE_BLOCK
    slot_tok = jnp.zeros((n_slots,), jnp.int32).at[dest].set(sorted_tok)
    block_expert = jnp.minimum(
        jnp.searchsorted(pad_end, jnp.arange(n_blocks, dtype=jnp.int32) * MOE_BLOCK, side='right'),
        N_EXPERTS - 1).astype(jnp.int32)
    x_blocks = xt[slot_tok].reshape(n_blocks, MOE_BLOCK, d)

    def expert_fn(args):
        xb, e = args
        return swiglu(xb, w_gate[e], w_up[e], w_down[e])

    y_blocks = lax.map(expert_fn, (x_blocks, block_expert))
    y_assign = y_blocks.reshape(n_slots, d)[dest] * sorted_w[:, None].astype(h.dtype)
    out = jnp.zeros((t, d), h.dtype).at[sorted_tok].add(y_assign)
    return out.reshape(bsz, s, d)


def setup_inputs(seed: int = 0) -> dict:
    key = jax.random.key(seed)
    ks = jax.random.split(key, 24)
    nrm = lambda k, shape, scale: jax.random.normal(k, shape, jnp.float32) * scale
    lam_u = jax.random.uniform(ks[9], (DEPTH, D_LRU), jnp.float32, 0.9, 0.999)
    lam_p = lam_u ** (1.0 / LRU_C)
    dt = jnp.exp(jax.random.uniform(ks[13], (DEPTH, DN_HEADS), jnp.float32, math.log(1e-3), math.log(1e-1)))
    return {
        'x': jax.random.normal(ks[0], (BATCH, SEQ, D_MODEL), jnp.float32),
        'norm_mix': 1.0 + nrm(ks[1], (DEPTH, D_MODEL), 0.02),
        'w_in': nrm(ks[2], (DEPTH, D_MODEL, D_IN), D_MODEL ** -0.5),
        'conv_lru_w': nrm(ks[3], (DEPTH, CONV_W, D_LRU), CONV_W ** -0.5),
        'conv_lru_b': nrm(ks[4], (DEPTH, D_LRU), 0.02),
        'lru_w_r': nrm(ks[5], (DEPTH, LRU_BLOCKS, LRU_BLOCK_W, LRU_BLOCK_W), LRU_BLOCK_W ** -0.5),
        'lru_b_r': nrm(ks[6], (DEPTH, LRU_BLOCKS, LRU_BLOCK_W), 0.02),
        'lru_w_i': nrm(ks[7], (DEPTH, LRU_BLOCKS, LRU_BLOCK_W, LRU_BLOCK_W), LRU_BLOCK_W ** -0.5),
        'lru_b_i': nrm(ks[8], (DEPTH, LRU_BLOCKS, LRU_BLOCK_W), 0.02),
        'lru_lambda': jnp.log(lam_p) - jnp.log1p(-lam_p),
        'lru_out_norm': 1.0 + nrm(ks[10], (DEPTH, D_LRU), 0.02),
        'conv_qkv_w': nrm(ks[11], (DEPTH, CONV_W, 3 * D_DN), CONV_W ** -0.5),
        'dn_a_log': jnp.log(jax.random.uniform(ks[12], (DEPTH, DN_HEADS), jnp.float32, 1.0, 16.0)),
        'dn_dt_bias': dt + jnp.log(-jnp.expm1(-dt)),
        'dn_out_norm': 1.0 + nrm(ks[14], (DEPTH, DN_HEAD_DIM), 0.02),
        'w_out': nrm(ks[15], (DEPTH, D_MIX, D_MODEL), D_MIX ** -0.5),
        'norm_ffn': 1.0 + nrm(ks[16], (DEPTH, D_MODEL), 0.02),
        'ffn_w_gate': nrm(ks[17], (N_DENSE, D_MODEL, D_FF), D_MODEL ** -0.5),
        'ffn_w_up': nrm(ks[18], (N_DENSE, D_MODEL, D_FF), D_MODEL ** -0.5),
        'ffn_w_down': nrm(ks[19], (N_DENSE, D_FF, D_MODEL), D_FF ** -0.5),
        'moe_router': nrm(ks[20], (N_MOE, D_MODEL, N_EXPERTS), D_MODEL ** -0.5),
        'moe_w_gate': nrm(ks[21], (N_MOE, N_EXPERTS, D_MODEL, D_FF_EXPERT), D_MODEL ** -0.5),
        'moe_w_up': nrm(ks[22], (N_MOE, N_EXPERTS, D_MODEL, D_FF_EXPERT), D_MODEL ** -0.5),
        'moe_w_down': nrm(ks[23], (N_MOE, N_EXPERTS, D_FF_EXPERT, D_MODEL), D_FF_EXPERT ** -0.5),
        'norm_final': 1.0 + nrm(jax.random.fold_in(key, 99), (D_MODEL,), 0.02),
    }


def reference(x, norm_mix, w_in, conv_lru_w, conv_lru_b, lru_w_r, lru_b_r, lru_w_i, lru_b_i,
              lru_lambda, lru_out_norm, conv_qkv_w, dn_a_log, dn_dt_bias, dn_out_norm, w_out,
              norm_ffn, ffn_w_gate, ffn_w_up, ffn_w_down, moe_router, moe_w_gate, moe_w_up,
              moe_w_down, norm_final):
    split_points = list(np.cumsum([D_LRU, D_LRU, D_DN, D_DN, D_DN, D_DN, DN_HEADS]))
    for l in range(DEPTH):
        h = rms_norm(x, norm_mix[l])
        proj = h @ w_in[l]
        lru_x, lru_g, dn_q, dn_k, dn_v, dn_g, dn_beta, dn_alpha = jnp.split(proj, split_points, axis=-1)
        y_lru = rg_lru_group(lru_x, lru_g, conv_lru_w[l], conv_lru_b[l], lru_w_r[l], lru_b_r[l],
                             lru_w_i[l], lru_b_i[l], lru_lambda[l], lru_out_norm[l])
        y_dn = gated_deltanet_group(dn_q, dn_k, dn_v, dn_g, dn_beta, dn_alpha, conv_qkv_w[l],
                                    dn_a_log[l], dn_dt_bias[l], dn_out_norm[l])
        y_mix = jnp.concatenate([y_lru.astype(x.dtype), y_dn.astype(x.dtype)], axis=-1)
        x = x + y_mix @ w_out[l]
        h2 = rms_norm(x, norm_ffn[l])
        if l % 2 == 0:
            j = l // 2
            x = x + swiglu(h2, ffn_w_gate[j], ffn_w_up[j], ffn_w_down[j])
        else:
            j = l // 2
            x = x + moe_swiglu(h2, moe_router[j], moe_w_gate[j], moe_w_up[j], moe_w_down[j])
    return rms_norm(x, norm_final)
```

```python
import functools

import jax
import jax.numpy as jnp
from jax import lax
from jax.experimental import pallas as pl
from jax.experimental.pallas import tpu as pltpu

F32 = jnp.float32
BF16 = jnp.bfloat16
I32 = jnp.int32

D_MODEL = 2048
D_LRU = 1024
LRU_BLOCKS = 8
LRU_BLOCK_W = 128
LRU_C = 8.0
CONV_W = 4
D_DN = 1024
DN_HEAD_DIM = 128
DN_HEADS = 8
DN_CHUNK = 64
D_PROJ = 2 * D_LRU + 4 * D_DN
N_EXPERTS = 8
EPS = 1e-6

LANES = 128
SUBLANES = 8
ROW_CHUNKS = D_MODEL // LANES
VMEM_LIMIT = 56 * 1024 * 1024


def _params(*sem):
    return pltpu.CompilerParams(dimension_semantics=sem, vmem_limit_bytes=VMEM_LIMIT)


def _rms(x, gain):
    ms = jnp.mean(x * x, axis=-1, keepdims=True)
    return x * lax.rsqrt(ms + EPS) * gain


def _softplus(x):
    return jnp.maximum(x, 0.0) + jnp.log1p(jnp.exp(-jnp.abs(x)))


def _dot(a, b):
    return jnp.dot(a, b, preferred_element_type=F32)


def _dot_nt(a, b, precision=None):
    return lax.dot_general(a, b, (((1,), (1,)), ((), ())),
                           preferred_element_type=F32, precision=precision)


def _dot_tn(a, b):
    return lax.dot_general(a, b, (((0,), (0,)), ((), ())), preferred_element_type=F32)


def _inproj_kernel(x_ref, g_ref, w_ref, ws_ref, wst_ref, o_ref, oc_ref, or_ref, h_ref):
    @pl.when(pl.program_id(1) == 0)
    def _():
        h = _rms(x_ref[...], g_ref[...]).astype(BF16)
        h_ref[...] = h
        oc_ref[...] = _dot(h, ws_ref[...])
        or_ref[...] = _dot_nt(wst_ref[...], h)

    o_ref[...] = _dot(h_ref[...], w_ref[...])


def _in_proj(x, gain, w_main, w_small, w_small_t, tm, tn):
    t, d = x.shape
    n = w_main.shape[1]
    return pl.pallas_call(
        _inproj_kernel,
        grid=(t // tm, n // tn),
        in_specs=[
            pl.BlockSpec((tm, d), lambda i, j: (i, 0)),
            pl.BlockSpec((1, d), lambda i, j: (0, 0)),
            pl.BlockSpec((d, tn), lambda i, j: (0, j)),
            pl.BlockSpec((d, LANES), lambda i, j: (0, 0)),
            pl.BlockSpec((2 * DN_HEADS, d), lambda i, j: (0, 0)),
        ],
        out_specs=[
            pl.BlockSpec((tm, tn), lambda i, j: (i, j)),
            pl.BlockSpec((tm, LANES), lambda i, j: (i, 0)),
            pl.BlockSpec((2 * DN_HEADS, tm), lambda i, j: (0, i)),
        ],
        out_shape=[
            jax.ShapeDtypeStruct((t, n), F32),
            jax.ShapeDtypeStruct((t, LANES), F32),
            jax.ShapeDtypeStruct((2 * DN_HEADS, t), F32),
        ],
        scratch_shapes=[pltpu.VMEM((tm, d), BF16)],
        compiler_params=_params("parallel", "arbitrary"),
        name="in_proj",
    )(x, gain, w_main, w_small, w_small_t)


def _lru_kernel(x_ref, gate_ref, cw_ref, cb_ref, wr_ref, br_ref, wi_ref, bi_ref, lam_ref,
                on_ref, o_ref, xs_ref, hc_ref, y_ref, *, tm):
    @pl.when(pl.program_id(1) == 0)
    def _():
        xs_ref[0:SUBLANES, :] = jnp.zeros((SUBLANES, D_LRU), F32)
        hc_ref[...] = jnp.zeros_like(hc_ref)

    xs_ref[SUBLANES:SUBLANES + tm, :] = x_ref[...]
    row = lax.broadcasted_iota(I32, (tm, LRU_BLOCK_W), 0)
    ssq = jnp.zeros((tm, 1), F32)
    for n in range(LRU_BLOCKS):
        sl = slice(n * LRU_BLOCK_W, (n + 1) * LRU_BLOCK_W)
        xc = xs_ref[SUBLANES - 3:SUBLANES - 3 + tm, sl] * cw_ref[0:1, sl]
        for k in range(1, CONV_W):
            off = SUBLANES - 3 + k
            xc = xc + xs_ref[off:off + tm, sl] * cw_ref[k:k + 1, sl]
        xc = xc + cb_ref[:, sl]
        xb = xc.astype(BF16)
        r = jax.nn.sigmoid(_dot(xb, wr_ref[n]) + br_ref[:, sl])
        i = jax.nn.sigmoid(_dot(xb, wi_ref[n]) + bi_ref[:, sl])
        log_a = -LRU_C * r * _softplus(-lam_ref[:, sl])
        a = jnp.exp(log_a)
        b = jnp.sqrt(-jnp.tanh(log_a) * (1.0 + a * a)) * (i * xc)
        d = 1
        while d < tm:
            keep = row >= d
            b = jnp.where(keep, a * pltpu.roll(b, d, 0) + b, b)
            a = jnp.where(keep, a * pltpu.roll(a, d, 0), a)
            d *= 2
        h = b + a * hc_ref[0:1, sl]
        hc_ref[0:1, sl] = h[tm - 1:tm, :]
        y = h * jax.nn.gelu(gate_ref[:, sl], approximate=True)
        y_ref[:, sl] = y
        ssq = ssq + jnp.sum(y * y, axis=-1, keepdims=True)
    inv = lax.rsqrt(ssq / D_LRU + EPS)
    o_ref[...] = (y_ref[...] * inv * on_ref[...]).astype(BF16)
    xs_ref[0:SUBLANES, :] = x_ref[tm - SUBLANES:tm, :]


def _rg_lru(proj, conv_w, conv_b, w_r, b_r, w_i, b_i, lam, out_norm, bsz, tm):
    t = proj.shape[0]
    ns = t // bsz // tm
    row = lambda b, s: (0, 0)
    return pl.pallas_call(
        functools.partial(_lru_kernel, tm=tm),
        grid=(bsz, ns),
        in_specs=[
            pl.BlockSpec((tm, D_LRU), lambda b, s: (b * ns + s, 0)),
            pl.BlockSpec((tm, D_LRU), lambda b, s: (b * ns + s, 1)),
            pl.BlockSpec((CONV_W, D_LRU), row),
            pl.BlockSpec((1, D_LRU), row),
            pl.BlockSpec((LRU_BLOCKS, LRU_BLOCK_W, LRU_BLOCK_W), lambda b, s: (0, 0, 0)),
            pl.BlockSpec((1, D_LRU), row),
            pl.BlockSpec((LRU_BLOCKS, LRU_BLOCK_W, LRU_BLOCK_W), lambda b, s: (0, 0, 0)),
            pl.BlockSpec((1, D_LRU), row),
            pl.BlockSpec((1, D_LRU), row),
            pl.BlockSpec((1, D_LRU), row),
        ],
        out_specs=pl.BlockSpec((tm, D_LRU), lambda b, s: (b * ns + s, 0)),
        out_shape=jax.ShapeDtypeStruct((t, D_LRU), BF16),
        scratch_shapes=[
            pltpu.VMEM((tm + SUBLANES, D_LRU), F32),
            pltpu.VMEM((SUBLANES, D_LRU), F32),
            pltpu.VMEM((tm, D_LRU), F32),
        ],
        compiler_params=_params("parallel", "arbitrary"),
        name="rg_lru",
    )(proj, proj, conv_w, conv_b, w_r, b_r, w_i, b_i, lam, out_norm)


def _gdn_kernel(q_ref, k_ref, v_ref, gate_ref, cwq_ref, cwk_ref, cwv_ref, bac_ref, bar_ref,
                alc_ref, dtc_ref, alr_ref, dtr_ref, on_ref, o_ref,
                qs_ref, ks_ref, vs_ref, st_ref, gr_ref, *, tm):
    hd = pl.program_id(1)
    c = DN_CHUNK

    @pl.when(pl.program_id(2) == 0)
    def _():
        zeros = jnp.zeros((SUBLANES, DN_HEAD_DIM), F32)
        qs_ref[0:SUBLANES, :] = zeros
        ks_ref[0:SUBLANES, :] = zeros
        vs_ref[0:SUBLANES, :] = zeros
        st_ref[...] = jnp.zeros_like(st_ref)

    def conv_silu(x_ref, xs_ref, cw_ref):
        xs_ref[SUBLANES:SUBLANES + tm, :] = x_ref[...]
        y = xs_ref[SUBLANES - 3:SUBLANES - 3 + tm, :] * cw_ref[0:1, :]
        for kk in range(1, CONV_W):
            off = SUBLANES - 3 + kk
            y = y + xs_ref[off:off + tm, :] * cw_ref[kk:kk + 1, :]
        xs_ref[0:SUBLANES, :] = x_ref[tm - SUBLANES:tm, :]
        return jax.nn.silu(y)

    q = conv_silu(q_ref, qs_ref, cwq_ref)
    k = conv_silu(k_ref, ks_ref, cwk_ref)
    v = conv_silu(v_ref, vs_ref, cwv_ref)
    q = q * lax.rsqrt(jnp.sum(q * q, axis=-1, keepdims=True) + EPS) * (DN_HEAD_DIM ** -0.5)
    k = k * lax.rsqrt(jnp.sum(k * k, axis=-1, keepdims=True) + EPS)

    bac = bac_ref[...]
    lane = lax.broadcasted_iota(I32, (tm, LANES), 1)
    rowc = lax.broadcasted_iota(I32, (tm, LANES), 0) % c
    g_all = -jnp.exp(alc_ref[...]) * _softplus(bac + dtc_ref[...])
    d = 1
    while d < c:
        g_all = g_all + jnp.where(rowc >= d, pltpu.roll(g_all, d, 0), 0.0)
        d *= 2
    gcum = jnp.sum(jnp.where(lane == hd + DN_HEADS, g_all, 0.0), axis=1, keepdims=True)
    beta = jnp.sum(jnp.where(lane == hd, jax.nn.sigmoid(bac), 0.0), axis=1, keepdims=True)

    g_row = -jnp.exp(alr_ref[...]) * _softplus(bar_ref[DN_HEADS:2 * DN_HEADS, :] + dtr_ref[...])
    lanec = lax.broadcasted_iota(I32, (DN_HEADS, tm), 1) % c
    d = 1
    while d < c:
        g_row = g_row + jnp.where(lanec >= d, pltpu.roll(g_row, d, 1), 0.0)
        d *= 2
    gr_ref[...] = g_row
    grow_all = gr_ref[pl.ds(hd, 1), :]

    ri = lax.broadcasted_iota(I32, (c, c), 0)
    ci = lax.broadcasted_iota(I32, (c, c), 1)
    causal = ri >= ci
    strict = ri > ci
    eye = (ri == ci).astype(F32)
    on = on_ref[...]
    state = st_ref[...]
    for ch in range(tm // c):
        rs = slice(ch * c, (ch + 1) * c)
        qc, kc, vc = q[rs], k[rs], v[rs]
        gcol = gcum[rs]
        grow = grow_all[:, rs]
        bcol = beta[rs]
        decay = jnp.where(causal, jnp.exp(jnp.where(causal, gcol - grow, 0.0)), 0.0)
        kb = kc * bcol
        vb = vc * bcol
        eg = jnp.exp(gcol)
        glast = gcol[c - 1:c, :]
        prod = _dot_nt(jnp.concatenate([qc, kb], axis=0).astype(BF16), kc.astype(BF16))
        qk = jnp.where(causal, prod[0:c] * decay, 0.0)
        low = jnp.where(strict, prod[c:2 * c] * decay, 0.0)
        inv = eye - low
        pw = low
        n_sq = 1
        while 2 * n_sq < c:
            pwb = pw.astype(BF16)
            pw = _dot(pwb, pwb)
            inv = inv + _dot(inv.astype(BF16), pw.astype(BF16))
            n_sq *= 2
        rhs = jnp.concatenate([vb, kb * eg], axis=1).astype(BF16)
        sol = _dot(inv.astype(BF16), rhs)
        u = sol[:, 0:DN_HEAD_DIM]
        w = sol[:, DN_HEAD_DIM:2 * DN_HEAD_DIM]
        q_dec = qc * eg
        k_dec = kc * jnp.exp(glast - gcol)
        ws = _dot(jnp.concatenate([w, q_dec], axis=0).astype(BF16), state.astype(BF16))
        v_new = u - ws[0:c]
        v_nb = v_new.astype(BF16)
        o = ws[c:2 * c] + _dot(qk.astype(BF16), v_nb)
        state = state * jnp.exp(glast) + _dot_tn(k_dec.astype(BF16), v_nb)
        gate = gate_ref[rs, :]
        o_ref[rs, :] = (_rms(o, on) * jax.nn.silu(gate)).astype(BF16)
    st_ref[...] = state


def _gated_deltanet(proj, ba_col, ba_row, conv_w, a_log, dt_bias, out_norm, bsz, tm):
    t = proj.shape[0]
    ns = t // bsz // tm
    hh = DN_HEADS
    q0 = 2 * D_LRU // DN_HEAD_DIM
    pad = jnp.zeros((hh,), F32)
    al_c = jnp.concatenate([pad, a_log, jnp.zeros((LANES - 2 * hh,), F32)]).reshape(1, LANES)
    dt_c = jnp.concatenate([pad, dt_bias, jnp.zeros((LANES - 2 * hh,), F32)]).reshape(1, LANES)
    al_r = a_log.reshape(hh, 1)
    dt_r = dt_bias.reshape(hh, 1)
    tok = lambda col0: (lambda b, h, s: (b * ns + s, col0 + h))
    cw = lambda col0: (lambda b, h, s: (0, col0 + h))
    const = lambda b, h, s: (0, 0)
    return pl.pallas_call(
        functools.partial(_gdn_kernel, tm=tm),
        grid=(bsz, hh, ns),
        in_specs=[
            pl.BlockSpec((tm, DN_HEAD_DIM), tok(q0)),
            pl.BlockSpec((tm, DN_HEAD_DIM), tok(q0 + hh)),
            pl.BlockSpec((tm, DN_HEAD_DIM), tok(q0 + 2 * hh)),
            pl.BlockSpec((tm, DN_HEAD_DIM), tok(q0 + 3 * hh)),
            pl.BlockSpec((CONV_W, DN_HEAD_DIM), cw(0)),
            pl.BlockSpec((CONV_W, DN_HEAD_DIM), cw(hh)),
            pl.BlockSpec((CONV_W, DN_HEAD_DIM), cw(2 * hh)),
            pl.BlockSpec((tm, LANES), lambda b, h, s: (b * ns + s, 0)),
            pl.BlockSpec((2 * hh, tm), lambda b, h, s: (0, b * ns + s)),
            pl.BlockSpec((1, LANES), const),
            pl.BlockSpec((1, LANES), const),
            pl.BlockSpec((hh, 1), const),
            pl.BlockSpec((hh, 1), const),
            pl.BlockSpec((1, DN_HEAD_DIM), const),
        ],
        out_specs=pl.BlockSpec((tm, DN_HEAD_DIM), lambda b, h, s: (b * ns + s, h)),
        out_shape=jax.ShapeDtypeStruct((t, D_DN), BF16),
        scratch_shapes=[
            pltpu.VMEM((tm + SUBLANES, DN_HEAD_DIM), F32),
            pltpu.VMEM((tm + SUBLANES, DN_HEAD_DIM), F32),
            pltpu.VMEM((tm + SUBLANES, DN_HEAD_DIM), F32),
            pltpu.VMEM((DN_HEAD_DIM, DN_HEAD_DIM), F32),
            pltpu.VMEM((hh, tm), F32),
        ],
        compiler_params=_params("parallel", "parallel", "arbitrary"),
        name="gated_deltanet",
    )(proj, proj, proj, proj, conv_w, conv_w, conv_w, ba_col, ba_row,
      al_c, dt_c, al_r, dt_r, out_norm)


def _outproj_kernel(yl_ref, yd_ref, w_ref, x_ref, g_ref, xo_ref, h_ref, *, rows_as_tiles):
    acc = _dot(yl_ref[...], w_ref[0:D_LRU, :]) + _dot(yd_ref[...], w_ref[D_LRU:D_LRU + D_DN, :])
    xn = x_ref[...] + acc
    xo_ref[...] = xn
    hn = _rms(xn, g_ref[...])
    if rows_as_tiles:
        for cc in range(ROW_CHUNKS):
            h_ref[:, cc, :] = hn[:, cc * LANES:(cc + 1) * LANES]
    else:
        h_ref[...] = hn.astype(h_ref.dtype)


def _out_proj(y_lru, y_dn, w, x, gain, tm, rows_as_tiles):
    t, d = x.shape
    if rows_as_tiles:
        h_spec = pl.BlockSpec((tm, ROW_CHUNKS, LANES), lambda i: (i, 0, 0))
        h_shape = jax.ShapeDtypeStruct((t, ROW_CHUNKS, LANES), F32)
    else:
        h_spec = pl.BlockSpec((tm, d), lambda i: (i, 0))
        h_shape = jax.ShapeDtypeStruct((t, d), BF16)
    return pl.pallas_call(
        functools.partial(_outproj_kernel, rows_as_tiles=rows_as_tiles),
        grid=(t // tm,),
        in_specs=[
            pl.BlockSpec((tm, D_LRU), lambda i: (i, 0)),
            pl.BlockSpec((tm, D_DN), lambda i: (i, 0)),
            pl.BlockSpec((D_LRU + D_DN, d), lambda i: (0, 0)),
            pl.BlockSpec((tm, d), lambda i: (i, 0)),
            pl.BlockSpec((1, d), lambda i: (0, 0)),
        ],
        out_specs=[pl.BlockSpec((tm, d), lambda i: (i, 0)), h_spec],
        out_shape=[jax.ShapeDtypeStruct((t, d), F32), h_shape],
        compiler_params=_params("parallel"),
        name="out_proj",
    )(y_lru, y_dn, w, x, gain)


def _ffn_kernel(h_ref, x_ref, wg_ref, wu_ref, wd_ref, o_ref, acc_ref):
    f = pl.program_id(1)
    h = h_ref[...]
    g = _dot(h, wg_ref[...])
    u = _dot(h, wu_ref[...])
    part = _dot((jax.nn.silu(g) * u).astype(BF16), wd_ref[...])

    @pl.when(f == 0)
    def _():
        acc_ref[...] = part

    @pl.when(f > 0)
    def _():
        acc_ref[...] += part

    @pl.when(f == pl.num_programs(1) - 1)
    def _():
        o_ref[...] = x_ref[...] + acc_ref[...]


def _ffn_dense(h, x, wg, wu, wd, tm, tf):
    t, d = x.shape
    ff = wg.shape[1]
    return pl.pallas_call(
        _ffn_kernel,
        grid=(t // tm, ff // tf),
        in_specs=[
            pl.BlockSpec((tm, d), lambda i, f: (i, 0)),
            pl.BlockSpec((tm, d), lambda i, f: (i, 0)),
            pl.BlockSpec((d, tf), lambda i, f: (0, f)),
            pl.BlockSpec((d, tf), lambda i, f: (0, f)),
            pl.BlockSpec((tf, d), lambda i, f: (f, 0)),
        ],
        out_specs=pl.BlockSpec((tm, d), lambda i, f: (i, 0)),
        out_shape=jax.ShapeDtypeStruct((t, d), F32),
        scratch_shapes=[pltpu.VMEM((tm, d), F32)],
        compiler_params=_params("parallel", "arbitrary"),
        name="ffn_dense",
    )(h, x, wg, wu, wd)


R_E1, R_E2, R_W1, R_W2, R_RANK1, R_RANK2 = range(6)


def _router_kernel(x_ref, g_ref, rt_ref, mr_ref, mc_ref, cnt_ref, carry_ref, *, tm):
    @pl.when(pl.program_id(0) == 0)
    def _():
        carry_ref[...] = jnp.zeros_like(carry_ref)

    h = _rms(x_ref[...], g_ref[...])
    logits = _dot_nt(rt_ref[...], h, precision=lax.Precision.HIGHEST)
    eidx = lax.broadcasted_iota(I32, (N_EXPERTS, tm), 0)
    m1 = jnp.max(logits, axis=0, keepdims=True)
    i1 = jnp.min(jnp.where(logits == m1, eidx, N_EXPERTS), axis=0, keepdims=True)
    rest = jnp.where(eidx == i1, -jnp.inf, logits)
    m2 = jnp.max(rest, axis=0, keepdims=True)
    i2 = jnp.min(jnp.where(rest == m2, eidx, N_EXPERTS), axis=0, keepdims=True)
    e2 = jnp.exp(m2 - m1)
    w1 = 1.0 / (1.0 + e2)
    w2 = e2 / (1.0 + e2)
    oh1 = eidx == i1
    oh2 = eidx == i2
    cnt = jnp.where(oh1 | oh2, 1.0, 0.0)
    ti = lax.broadcasted_iota(I32, (tm, tm), 0)
    tj = lax.broadcasted_iota(I32, (tm, tm), 1)
    upper = jnp.where(ti <= tj, 1.0, 0.0).astype(BF16)
    cum = _dot(cnt.astype(BF16), upper)
    before = cum - cnt + carry_ref[:, 0:1]
    rank1 = jnp.sum(jnp.where(oh1, before, 0.0), axis=0, keepdims=True)
    rank2 = jnp.sum(jnp.where(oh2, before, 0.0), axis=0, keepdims=True)
    total = carry_ref[:, 0:1] + cum[:, tm - 1:tm]
    carry_ref[...] = jnp.broadcast_to(total, carry_ref.shape)
    cnt_ref[...] = jnp.broadcast_to(total, cnt_ref.shape).astype(I32)
    rec = jnp.concatenate(
        [i1.astype(F32), i2.astype(F32), w1, w2, rank1, rank2,
         jnp.zeros((LANES - 6, tm), F32)], axis=0)
    mr_ref[...] = rec[0:SUBLANES, :]
    mc_ref[...] = rec.T


def _router(x, gain, router_t, tm):
    t, d = x.shape
    return pl.pallas_call(
        functools.partial(_router_kernel, tm=tm),
        grid=(t // tm,),
        in_specs=[
            pl.BlockSpec((tm, d), lambda i: (i, 0)),
            pl.BlockSpec((1, d), lambda i: (0, 0)),
            pl.BlockSpec((N_EXPERTS, d), lambda i: (0, 0)),
        ],
        out_specs=[
            pl.BlockSpec((SUBLANES, tm), lambda i: (0, i)),
            pl.BlockSpec((tm, LANES), lambda i: (i, 0)),
            pl.BlockSpec((N_EXPERTS, LANES), lambda i: (0, 0)),
        ],
        out_shape=[
            jax.ShapeDtypeStruct((SUBLANES, t), F32),
            jax.ShapeDtypeStruct((t, LANES), F32),
            jax.ShapeDtypeStruct((N_EXPERTS, LANES), I32),
        ],
        scratch_shapes=[pltpu.VMEM((N_EXPERTS, LANES), F32)],
        compiler_params=_params("arbitrary"),
        name="moe_router",
    )(x, gain, router_t)


def _dispatch_kernel(d1_ref, d2_ref, h_ref, xs_in_ref, xs_ref, sem, *, tm):
    del xs_in_ref
    base = pl.program_id(0) * tm

    def issue(r, carry):
        pltpu.make_async_copy(h_ref.at[r], xs_ref.at[d1_ref[base + r]], sem.at[0]).start()
        pltpu.make_async_copy(h_ref.at[r], xs_ref.at[d2_ref[base + r]], sem.at[1]).start()
        return carry

    lax.fori_loop(0, tm, issue, 0)
    pltpu.make_async_copy(h_ref, xs_ref.at[pl.ds(0, tm)], sem.at[0]).wait()
    pltpu.make_async_copy(h_ref, xs_ref.at[pl.ds(0, tm)], sem.at[1]).wait()


def _dispatch(dest1, dest2, h_tiles, n_slots, tm):
    t = h_tiles.shape[0]
    xs0 = jnp.zeros((n_slots, ROW_CHUNKS, LANES), F32)
    return pl.pallas_call(
        functools.partial(_dispatch_kernel, tm=tm),
        grid_spec=pltpu.PrefetchScalarGridSpec(
            num_scalar_prefetch=2,
            grid=(t // tm,),
            in_specs=[
                pl.BlockSpec((tm, ROW_CHUNKS, LANES), lambda i, d1, d2: (i, 0, 0)),
                pl.BlockSpec(memory_space=pl.ANY),
            ],
            out_specs=pl.BlockSpec(memory_space=pl.ANY),
            scratch_shapes=[pltpu.SemaphoreType.DMA((2,))],
        ),
        out_shape=jax.ShapeDtypeStruct((n_slots, ROW_CHUNKS, LANES), F32),
        input_output_aliases={3: 0},
        compiler_params=_params("arbitrary"),
        name="moe_dispatch",
    )(dest1, dest2, h_tiles, xs0)


def _gmm_kernel(be_ref, nu_ref, xs_ref, wg_ref, wu_ref, wd_ref, y_ref, xb_ref, acc_ref):
    del be_ref
    f = pl.program_id(1)
    used = pl.program_id(0) < nu_ref[0]

    @pl.when(jnp.logical_and(jnp.logical_not(used), f == pl.num_programs(1) - 1))
    def _():
        y_ref[...] = jnp.zeros_like(y_ref)

    @pl.when(used)
    def _():
        @pl.when(f == 0)
        def _():
            xb_ref[...] = jnp.concatenate(
                [xs_ref[:, cc, :] for cc in range(ROW_CHUNKS)], axis=1).astype(BF16)

        xb = xb_ref[...]
        g = _dot(xb, wg_ref[...])
        u = _dot(xb, wu_ref[...])
        part = _dot((jax.nn.silu(g) * u).astype(BF16), wd_ref[...])

        @pl.when(f == 0)
        def _():
            acc_ref[...] = part

        @pl.when(f > 0)
        def _():
            acc_ref[...] += part

        @pl.when(f == pl.num_programs(1) - 1)
        def _():
            for cc in range(ROW_CHUNKS):
                y_ref[:, cc, :] = acc_ref[:, cc * LANES:(cc + 1) * LANES]


def _gmm(block_expert, n_used, xs, wg, wu, wd, tm, tf):
    n_slots = xs.shape[0]
    d = D_MODEL
    ff = wg.shape[2]
    nf = ff // tf

    def blk(b, nu):
        return jnp.minimum(b, nu[0] - 1)

    def fidx(b, f, nu):
        return jnp.where(b < nu[0], f, nf - 1)

    return pl.pallas_call(
        _gmm_kernel,
        grid_spec=pltpu.PrefetchScalarGridSpec(
            num_scalar_prefetch=2,
            grid=(n_slots // tm, nf),
            in_specs=[
                pl.BlockSpec((tm, ROW_CHUNKS, LANES), lambda b, f, be, nu: (blk(b, nu), 0, 0)),
                pl.BlockSpec((None, d, tf), lambda b, f, be, nu: (be[blk(b, nu)], 0, fidx(b, f, nu))),
                pl.BlockSpec((None, d, tf), lambda b, f, be, nu: (be[blk(b, nu)], 0, fidx(b, f, nu))),
                pl.BlockSpec((None, tf, d), lambda b, f, be, nu: (be[blk(b, nu)], fidx(b, f, nu), 0)),
            ],
            out_specs=pl.BlockSpec((tm, ROW_CHUNKS, LANES), lambda b, f, be, nu: (b, 0, 0)),
            scratch_shapes=[pltpu.VMEM((tm, d), BF16), pltpu.VMEM((tm, d), F32)],
        ),
        out_shape=jax.ShapeDtypeStruct((n_slots, ROW_CHUNKS, LANES), F32),
        compiler_params=_params("arbitrary", "arbitrary"),
        name="moe_gmm",
    )(block_expert, n_used, xs, wg, wu, wd)


def _combine_kernel(d1_ref, d2_ref, x_ref, mc_ref, y_ref, g_ref, o_ref, b1_ref, b2_ref, sem, *, tm):
    base = pl.program_id(0) * tm

    def issue(r, carry):
        pltpu.make_async_copy(y_ref.at[d1_ref[base + r]], b1_ref.at[r], sem.at[0]).start()
        pltpu.make_async_copy(y_ref.at[d2_ref[base + r]], b2_ref.at[r], sem.at[1]).start()
        return carry

    lax.fori_loop(0, tm, issue, 0)
    pltpu.make_async_copy(y_ref.at[pl.ds(0, tm)], b1_ref, sem.at[0]).wait()
    pltpu.make_async_copy(y_ref.at[pl.ds(0, tm)], b2_ref, sem.at[1]).wait()
    w1 = mc_ref[:, R_W1:R_W1 + 1]
    w2 = mc_ref[:, R_W2:R_W2 + 1]
    xn = jnp.concatenate(
        [x_ref[:, cc * LANES:(cc + 1) * LANES] + (w1 * b1_ref[:, cc, :] + w2 * b2_ref[:, cc, :])
         for cc in range(ROW_CHUNKS)], axis=1)
    o_ref[...] = _rms(xn, g_ref[...])


def _combine(dest1, dest2, x, meta_col, y, gain, tm):
    t, d = x.shape
    return pl.pallas_call(
        functools.partial(_combine_kernel, tm=tm),
        grid_spec=pltpu.PrefetchScalarGridSpec(
            num_scalar_prefetch=2,
            grid=(t // tm,),
            in_specs=[
                pl.BlockSpec((tm, d), lambda i, d1, d2: (i, 0)),
                pl.BlockSpec((tm, LANES), lambda i, d1, d2: (i, 0)),
                pl.BlockSpec(memory_space=pl.ANY),
                pl.BlockSpec((1, d), lambda i, d1, d2: (0, 0)),
            ],
            out_specs=pl.BlockSpec((tm, d), lambda i, d1, d2: (i, 0)),
            scratch_shapes=[
                pltpu.VMEM((tm, ROW_CHUNKS, LANES), F32),
                pltpu.VMEM((tm, ROW_CHUNKS, LANES), F32),
                pltpu.SemaphoreType.DMA((2,)),
            ],
        ),
        out_shape=jax.ShapeDtypeStruct((t, d), F32),
        compiler_params=_params("arbitrary"),
        name="moe_combine",
    )(dest1, dest2, x, meta_col, y, gain)


def _tiles(t, seq):
    pick = lambda want, total: want if total % want == 0 else total
    return dict(
        proj_tm=pick(512, t), proj_tn=1024,
        lru_tm=pick(256, seq),
        gdn_tm=pick(512, seq),
        out_tm=pick(512, t),
        ffn_tm=pick(512, t), ffn_tf=512,
        router_tm=pick(512, t),
        dispatch_tm=pick(256, t),
        gmm_tm=512, gmm_tf=512,
        combine_tm=pick(256, t),
    )


def _moe(x, h_tiles, norm_ffn, router, wg, wu, wd, norm_final, tl):
    t = x.shape[0]
    tmb = tl["gmm_tm"]
    meta_row, meta_col, counts = _router(x, norm_ffn, router.T, tl["router_tm"])
    counts = counts[:, 0]
    padded = (counts + tmb - 1) // tmb * tmb
    pad_end = jnp.cumsum(padded)
    pad_start = pad_end - padded
    e1 = meta_row[R_E1].astype(I32)
    e2 = meta_row[R_E2].astype(I32)
    experts = jnp.arange(N_EXPERTS, dtype=I32)[:, None]
    start1 = jnp.sum(jnp.where(e1[None, :] == experts, pad_start[:, None], 0), axis=0)
    start2 = jnp.sum(jnp.where(e2[None, :] == experts, pad_start[:, None], 0), axis=0)
    dest1 = start1 + meta_row[R_RANK1].astype(I32)
    dest2 = start2 + meta_row[R_RANK2].astype(I32)
    n_slots = 2 * t + N_EXPERTS * tmb
    n_blocks = n_slots // tmb
    block_start = jnp.arange(n_blocks, dtype=I32) * tmb
    block_expert = jnp.minimum(
        jnp.sum((block_start[:, None] >= pad_end[None, :]).astype(I32), axis=1), N_EXPERTS - 1)
    n_used = (pad_end[-1] // tmb).astype(I32).reshape(1)
    xs = _dispatch(dest1, dest2, h_tiles, n_slots, tl["dispatch_tm"])
    y = _gmm(block_expert, n_used, xs, wg, wu, wd, tmb, tl["gmm_tf"])
    return _combine(dest1, dest2, x, meta_col, y, norm_final, tl["combine_tm"])


def kernel(x, norm_mix, w_in, conv_lru_w, conv_lru_b, lru_w_r, lru_b_r, lru_w_i, lru_b_i,
           lru_lambda, lru_out_norm, conv_qkv_w, dn_a_log, dn_dt_bias, dn_out_norm, w_out,
           norm_ffn, ffn_w_gate, ffn_w_up, ffn_w_down, moe_router, moe_w_gate, moe_w_up,
           moe_w_down, norm_final):
    bsz, seq, d = x.shape
    t = bsz * seq
    depth = w_in.shape[0]
    tl = _tiles(t, seq)
    row = lambda v: v.reshape(1, -1)
    xt = x.reshape(t, d)
    out = None
    for l in range(depth):
        is_moe = l % 2 == 1
        w_main = w_in[l, :, :D_PROJ].astype(BF16)
        w_small = w_in[l, :, D_PROJ:]
        w_small_p = jnp.pad(w_small, ((0, 0), (0, LANES - 2 * DN_HEADS))).astype(BF16)
        proj, ba_col, ba_row = _in_proj(xt, row(norm_mix[l]), w_main, w_small_p,
                                        w_small.T.astype(BF16), tl["proj_tm"], tl["proj_tn"])
        y_lru = _rg_lru(proj, conv_lru_w[l], row(conv_lru_b[l]), lru_w_r[l].astype(BF16),
                        row(lru_b_r[l]), lru_w_i[l].astype(BF16), row(lru_b_i[l]),
                        row(lru_lambda[l]), row(lru_out_norm[l]), bsz, tl["lru_tm"])
        y_dn = _gated_deltanet(proj, ba_col, ba_row, conv_qkv_w[l], dn_a_log[l], dn_dt_bias[l],
                               row(dn_out_norm[l]), bsz, tl["gdn_tm"])
        xt, h2 = _out_proj(y_lru, y_dn, w_out[l].astype(BF16), xt, row(norm_ffn[l]),
                           tl["out_tm"], rows_as_tiles=is_moe)
        j = l // 2
        if not is_moe:
            xt = _ffn_dense(h2, xt, ffn_w_gate[j].astype(BF16), ffn_w_up[j].astype(BF16),
                            ffn_w_down[j].astype(BF16), tl["ffn_tm"], tl["ffn_tf"])
        else:
            assert l == depth - 1, "the routed layer fuses the final RMSNorm"
            out = _moe(xt, h2, row(norm_ffn[l]), moe_router[j], moe_w_gate[j].astype(BF16),
                       moe_w_up[j].astype(BF16), moe_w_down[j].astype(BF16), row(norm_final), tl)
    return out.reshape(bsz, seq, d)
```

```python
import functools

import jax
import jax.numpy as jnp
from jax import lax
from jax.experimental import pallas as pl
from jax.experimental.pallas import tpu as pltpu

F32 = jnp.float32
BF16 = jnp.bfloat16
I32 = jnp.int32

D_MODEL = 2048
D_LRU = 1024
LRU_BLOCKS = 8
LRU_BLOCK_W = 128
LRU_C = 8.0
CONV_W = 4
D_DN = 1024
DN_HEAD_DIM = 128
DN_HEADS = 8
DN_CHUNK = 64
D_PROJ = 2 * D_LRU + 4 * D_DN
N_EXPERTS = 8
EPS = 1e-6

LANES = 128
SUBLANES = 8
ROW_CHUNKS = D_MODEL // LANES
VMEM_LIMIT = 56 * 1024 * 1024


def _params(*sem):
    return pltpu.CompilerParams(dimension_semantics=sem, vmem_limit_bytes=VMEM_LIMIT)


def _rms(x, gain):
    ms = jnp.mean(x * x, axis=-1, keepdims=True)
    return x * lax.rsqrt(ms + EPS) * gain


def _softplus(x):
    return jnp.maximum(x, 0.0) + jnp.log1p(jnp.exp(-jnp.abs(x)))


def _dot(a, b):
    return jnp.dot(a, b, preferred_element_type=F32)


def _dot_nt(a, b, precision=None):
    return lax.dot_general(a, b, (((1,), (1,)), ((), ())),
                           preferred_element_type=F32, precision=precision)


def _dot_tn(a, b):
    return lax.dot_general(a, b, (((0,), (0,)), ((), ())), preferred_element_type=F32)


def _inproj_kernel(x_ref, g_ref, w_ref, ws_ref, wst_ref, o_ref, oc_ref, or_ref, h_ref):
    @pl.when(pl.program_id(1) == 0)
    def _():
        h = _rms(x_ref[...], g_ref[...]).astype(BF16)
        h_ref[...] = h
        oc_ref[...] = _dot(h, ws_ref[...])
        or_ref[...] = _dot_nt(wst_ref[...], h)

    o_ref[...] = _dot(h_ref[...], w_ref[...])


def _in_proj(x, gain, w_main, w_small, w_small_t, tm, tn):
    t, d = x.shape
    n = w_main.shape[1]
    return pl.pallas_call(
        _inproj_kernel,
        grid=(t // tm, n // tn),
        in_specs=[
            pl.BlockSpec((tm, d), lambda i, j: (i, 0)),
            pl.BlockSpec((1, d), lambda i, j: (0, 0)),
            pl.BlockSpec((d, tn), lambda i, j: (0, j)),
            pl.BlockSpec((d, LANES), lambda i, j: (0, 0)),
            pl.BlockSpec((2 * DN_HEADS, d), lambda i, j: (0, 0)),
        ],
        out_specs=[
            pl.BlockSpec((tm, tn), lambda i, j: (i, j)),
            pl.BlockSpec((tm, LANES), lambda i, j: (i, 0)),
            pl.BlockSpec((2 * DN_HEADS, tm), lambda i, j: (0, i)),
        ],
        out_shape=[
            jax.ShapeDtypeStruct((t, n), F32),
            jax.ShapeDtypeStruct((t, LANES), F32),
            jax.ShapeDtypeStruct((2 * DN_HEADS, t), F32),
        ],
        scratch_shapes=[pltpu.VMEM((tm, d), BF16)],
        compiler_params=_params("parallel", "arbitrary"),
        name="in_proj",
    )(x, gain, w_main, w_small, w_small_t)


def _lru_kernel(x_ref, gate_ref, cw_ref, cb_ref, wr_ref, br_ref, wi_ref, bi_ref, lam_ref,
                on_ref, o_ref, xs_ref, hc_ref, y_ref, *, tm):
    @pl.when(pl.program_id(1) == 0)
    def _():
        xs_ref[0:SUBLANES, :] = jnp.zeros((SUBLANES, D_LRU), F32)
        hc_ref[...] = jnp.zeros_like(hc_ref)

    xs_ref[SUBLANES:SUBLANES + tm, :] = x_ref[...]
    row = lax.broadcasted_iota(I32, (tm, LRU_BLOCK_W), 0)
    ssq = jnp.zeros((tm, 1), F32)
    for n in range(LRU_BLOCKS):
        sl = slice(n * LRU_BLOCK_W, (n + 1) * LRU_BLOCK_W)
        xc = xs_ref[SUBLANES - 3:SUBLANES - 3 + tm, sl] * cw_ref[0:1, sl]
        for k in range(1, CONV_W):
            off = SUBLANES - 3 + k
            xc = xc + xs_ref[off:off + tm, sl] * cw_ref[k:k + 1, sl]
        xc = xc + cb_ref[:, sl]
        xb = xc.astype(BF16)
        r = jax.nn.sigmoid(_dot(xb, wr_ref[n]) + br_ref[:, sl])
        i = jax.nn.sigmoid(_dot(xb, wi_ref[n]) + bi_ref[:, sl])
        log_a = -LRU_C * r * _softplus(-lam_ref[:, sl])
        a = jnp.exp(log_a)
        b = jnp.sqrt(-jnp.tanh(log_a) * (1.0 + a * a)) * (i * xc)
        d = 1
        while d < tm:
            keep = row >= d
            b = jnp.where(keep, a * pltpu.roll(b, d, 0) + b, b)
            a = jnp.where(keep, a * pltpu.roll(a, d, 0), a)
            d *= 2
        h = b + a * hc_ref[0:1, sl]
        hc_ref[0:1, sl] = h[tm - 1:tm, :]
        y = h * jax.nn.gelu(gate_ref[:, sl], approximate=True)
        y_ref[:, sl] = y
        ssq = ssq + jnp.sum(y * y, axis=-1, keepdims=True)
    inv = lax.rsqrt(ssq / D_LRU + EPS)
    o_ref[...] = (y_ref[...] * inv * on_ref[...]).astype(BF16)
    xs_ref[0:SUBLANES, :] = x_ref[tm - SUBLANES:tm, :]


def _rg_lru(proj, conv_w, conv_b, w_r, b_r, w_i, b_i, lam, out_norm, bsz, tm):
    t = proj.shape[0]
    ns = t // bsz // tm
    row = lambda b, s: (0, 0)
    return pl.pallas_call(
        functools.partial(_lru_kernel, tm=tm),
        grid=(bsz, ns),
        in_specs=[
            pl.BlockSpec((tm, D_LRU), lambda b, s: (b * ns + s, 0)),
            pl.BlockSpec((tm, D_LRU), lambda b, s: (b * ns + s, 1)),
            pl.BlockSpec((CONV_W, D_LRU), row),
            pl.BlockSpec((1, D_LRU), row),
            pl.BlockSpec((LRU_BLOCKS, LRU_BLOCK_W, LRU_BLOCK_W), lambda b, s: (0, 0, 0)),
            pl.BlockSpec((1, D_LRU), row),
            pl.BlockSpec((LRU_BLOCKS, LRU_BLOCK_W, LRU_BLOCK_W), lambda b, s: (0, 0, 0)),
            pl.BlockSpec((1, D_LRU), row),
            pl.BlockSpec((1, D_LRU), row),
            pl.BlockSpec((1, D_LRU), row),
        ],
        out_specs=pl.BlockSpec((tm, D_LRU), lambda b, s: (b * ns + s, 0)),
        out_shape=jax.ShapeDtypeStruct((t, D_LRU), BF16),
        scratch_shapes=[
            pltpu.VMEM((tm + SUBLANES, D_LRU), F32),
            pltpu.VMEM((SUBLANES, D_LRU), F32),
            pltpu.VMEM((tm, D_LRU), F32),
        ],
        compiler_params=_params("parallel", "arbitrary"),
        name="rg_lru",
    )(proj, proj, conv_w, conv_b, w_r, b_r, w_i, b_i, lam, out_norm)


def _gdn_kernel(q_ref, k_ref, v_ref, gate_ref, cwq_ref, cwk_ref, cwv_ref, bac_ref, bar_ref,
                alc_ref, dtc_ref, alr_ref, dtr_ref, on_ref, o_ref,
                qs_ref, ks_ref, vs_ref, st_ref, gr_ref, *, tm, hb):
    hg = pl.program_id(1)
    c = DN_CHUNK

    @pl.when(pl.program_id(2) == 0)
    def _():
        zeros = jnp.zeros((SUBLANES, hb * DN_HEAD_DIM), F32)
        qs_ref[0:SUBLANES, :] = zeros
        ks_ref[0:SUBLANES, :] = zeros
        vs_ref[0:SUBLANES, :] = zeros
        st_ref[...] = jnp.zeros_like(st_ref)

    def conv_silu(x_ref, xs_ref, cw_ref):
        xs_ref[SUBLANES:SUBLANES + tm, :] = x_ref[...]
        y = xs_ref[SUBLANES - 3:SUBLANES - 3 + tm, :] * cw_ref[0:1, :]
        for kk in range(1, CONV_W):
            off = SUBLANES - 3 + kk
            y = y + xs_ref[off:off + tm, :] * cw_ref[kk:kk + 1, :]
        xs_ref[0:SUBLANES, :] = x_ref[tm - SUBLANES:tm, :]
        return jax.nn.silu(y)

    q_all = conv_silu(q_ref, qs_ref, cwq_ref)
    k_all = conv_silu(k_ref, ks_ref, cwk_ref)
    v_all = conv_silu(v_ref, vs_ref, cwv_ref)

    bac = bac_ref[...]
    lane = lax.broadcasted_iota(I32, (tm, LANES), 1)
    rowc = lax.broadcasted_iota(I32, (tm, LANES), 0) % c
    g_all = -jnp.exp(alc_ref[...]) * _softplus(bac + dtc_ref[...])
    d = 1
    while d < c:
        g_all = g_all + jnp.where(rowc >= d, pltpu.roll(g_all, d, 0), 0.0)
        d *= 2
    sig_all = jax.nn.sigmoid(bac)

    g_row = -jnp.exp(alr_ref[...]) * _softplus(bar_ref[DN_HEADS:2 * DN_HEADS, :] + dtr_ref[...])
    lanec = lax.broadcasted_iota(I32, (DN_HEADS, tm), 1) % c
    d = 1
    while d < c:
        g_row = g_row + jnp.where(lanec >= d, pltpu.roll(g_row, d, 1), 0.0)
        d *= 2
    gr_ref[...] = g_row

    ri = lax.broadcasted_iota(I32, (c, c), 0)
    ci = lax.broadcasted_iota(I32, (c, c), 1)
    causal = ri >= ci
    strict = ri > ci
    eye = (ri == ci).astype(F32)
    on = on_ref[...]
    n_ch = tm // c

    probs = []
    for j in range(hb):
        hd = hg * hb + j
        cs = slice(j * DN_HEAD_DIM, (j + 1) * DN_HEAD_DIM)
        q = q_all[:, cs]
        k = k_all[:, cs]
        q = q * lax.rsqrt(jnp.sum(q * q, axis=-1, keepdims=True) + EPS) * (DN_HEAD_DIM ** -0.5)
        k = k * lax.rsqrt(jnp.sum(k * k, axis=-1, keepdims=True) + EPS)
        gcum = jnp.sum(jnp.where(lane == hd + DN_HEADS, g_all, 0.0), axis=1, keepdims=True)
        beta = jnp.sum(jnp.where(lane == hd, sig_all, 0.0), axis=1, keepdims=True)
        grow_all = gr_ref[pl.ds(hd, 1), :]
        for ch in range(n_ch):
            rs = slice(ch * c, (ch + 1) * c)
            gcol = gcum[rs]
            kc = k[rs]
            kb = kc * beta[rs]
            probs.append(dict(
                j=j, ch=ch, qc=q[rs], kc=kc, kb=kb, vb=v_all[rs, cs] * beta[rs], gcol=gcol,
                eg=jnp.exp(gcol), glast=gcol[c - 1:c, :],
                decay=jnp.where(causal, jnp.exp(jnp.where(causal, gcol - grow_all[:, rs], 0.0)), 0.0)))
    for p in probs:
        p["prod"] = _dot_nt(jnp.concatenate([p["qc"], p["kb"]], axis=0).astype(BF16),
                            p["kc"].astype(BF16))
    for p in probs:
        p["qk"] = jnp.where(causal, p["prod"][0:c] * p["decay"], 0.0).astype(BF16)
        low = jnp.where(strict, p["prod"][c:2 * c] * p["decay"], 0.0)
        p["inv"] = eye - low
        p["pw"] = low.astype(BF16)
    n_sq = 1
    while 2 * n_sq < c:
        for p in probs:
            p["pw"] = _dot(p["pw"], p["pw"]).astype(BF16)
        for p in probs:
            p["inv"] = p["inv"] + _dot(p["inv"].astype(BF16), p["pw"])
        n_sq *= 2
    for p in probs:
        rhs = jnp.concatenate([p["vb"], p["kb"] * p["eg"]], axis=1).astype(BF16)
        sol = _dot(p["inv"].astype(BF16), rhs)
        p["u"] = sol[:, 0:DN_HEAD_DIM]
        w = sol[:, DN_HEAD_DIM:2 * DN_HEAD_DIM]
        p["wq"] = jnp.concatenate([w, p["qc"] * p["eg"]], axis=0).astype(BF16)
        p["k_dec"] = (p["kc"] * jnp.exp(p["glast"] - p["gcol"])).astype(BF16)
        p["g_end"] = jnp.exp(p["glast"])

    states = [st_ref[j] for j in range(hb)]
    for ch in range(n_ch):
        rs = slice(ch * c, (ch + 1) * c)
        cur = [probs[j * n_ch + ch] for j in range(hb)]
        wss = [_dot(p["wq"], states[j].astype(BF16)) for j, p in enumerate(cur)]
        v_nbs = [(p["u"] - ws[0:c]).astype(BF16) for p, ws in zip(cur, wss)]
        outs = [ws[c:2 * c] + _dot(p["qk"], v_nb) for p, ws, v_nb in zip(cur, wss, v_nbs)]
        states = [states[j] * p["g_end"] + _dot_tn(p["k_dec"], v_nb)
                  for j, (p, v_nb) in enumerate(zip(cur, v_nbs))]
        for j, o in enumerate(outs):
            cs = slice(j * DN_HEAD_DIM, (j + 1) * DN_HEAD_DIM)
            o_ref[rs, cs] = (_rms(o, on) * jax.nn.silu(gate_ref[rs, cs])).astype(BF16)
    for j in range(hb):
        st_ref[j] = states[j]


def _gated_deltanet(proj, ba_col, ba_row, conv_w, a_log, dt_bias, out_norm, bsz, tm, hb):
    t = proj.shape[0]
    ns = t // bsz // tm
    hh = DN_HEADS
    ng = hh // hb
    wb = hb * DN_HEAD_DIM
    q0 = 2 * D_LRU // wb
    pad = jnp.zeros((hh,), F32)
    al_c = jnp.concatenate([pad, a_log, jnp.zeros((LANES - 2 * hh,), F32)]).reshape(1, LANES)
    dt_c = jnp.concatenate([pad, dt_bias, jnp.zeros((LANES - 2 * hh,), F32)]).reshape(1, LANES)
    al_r = a_log.reshape(hh, 1)
    dt_r = dt_bias.reshape(hh, 1)
    tok = lambda col0: (lambda b, h, s: (b * ns + s, col0 + h))
    cw = lambda col0: (lambda b, h, s: (0, col0 + h))
    const = lambda b, h, s: (0, 0)
    return pl.pallas_call(
        functools.partial(_gdn_kernel, tm=tm, hb=hb),
        grid=(bsz, ng, ns),
        in_specs=[
            pl.BlockSpec((tm, wb), tok(q0)),
            pl.BlockSpec((tm, wb), tok(q0 + ng)),
            pl.BlockSpec((tm, wb), tok(q0 + 2 * ng)),
            pl.BlockSpec((tm, wb), tok(q0 + 3 * ng)),
            pl.BlockSpec((CONV_W, wb), cw(0)),
            pl.BlockSpec((CONV_W, wb), cw(ng)),
            pl.BlockSpec((CONV_W, wb), cw(2 * ng)),
            pl.BlockSpec((tm, LANES), lambda b, h, s: (b * ns + s, 0)),
            pl.BlockSpec((2 * hh, tm), lambda b, h, s: (0, b * ns + s)),
            pl.BlockSpec((1, LANES), const),
            pl.BlockSpec((1, LANES), const),
            pl.BlockSpec((hh, 1), const),
            pl.BlockSpec((hh, 1), const),
            pl.BlockSpec((1, DN_HEAD_DIM), const),
        ],
        out_specs=pl.BlockSpec((tm, wb), lambda b, h, s: (b * ns + s, h)),
        out_shape=jax.ShapeDtypeStruct((t, D_DN), BF16),
        scratch_shapes=[
            pltpu.VMEM((tm + SUBLANES, wb), F32),
            pltpu.VMEM((tm + SUBLANES, wb), F32),
            pltpu.VMEM((tm + SUBLANES, wb), F32),
            pltpu.VMEM((hb, DN_HEAD_DIM, DN_HEAD_DIM), F32),
            pltpu.VMEM((hh, tm), F32),
        ],
        compiler_params=_params("parallel", "parallel", "arbitrary"),
        name="gated_deltanet",
    )(proj, proj, proj, proj, conv_w, conv_w, conv_w, ba_col, ba_row,
      al_c, dt_c, al_r, dt_r, out_norm)


def _outproj_kernel(yl_ref, yd_ref, w_ref, x_ref, g_ref, xo_ref, h_ref, *, rows_as_tiles):
    acc = _dot(yl_ref[...], w_ref[0:D_LRU, :]) + _dot(yd_ref[...], w_ref[D_LRU:D_LRU + D_DN, :])
    xn = x_ref[...] + acc
    xo_ref[...] = xn
    hn = _rms(xn, g_ref[...])
    if rows_as_tiles:
        for cc in range(ROW_CHUNKS):
            h_ref[:, cc, :] = hn[:, cc * LANES:(cc + 1) * LANES]
    else:
        h_ref[...] = hn.astype(h_ref.dtype)


def _out_proj(y_lru, y_dn, w, x, gain, tm, rows_as_tiles):
    t, d = x.shape
    if rows_as_tiles:
        h_spec = pl.BlockSpec((tm, ROW_CHUNKS, LANES), lambda i: (i, 0, 0))
        h_shape = jax.ShapeDtypeStruct((t, ROW_CHUNKS, LANES), F32)
    else:
        h_spec = pl.BlockSpec((tm, d), lambda i: (i, 0))
        h_shape = jax.ShapeDtypeStruct((t, d), BF16)
    return pl.pallas_call(
        functools.partial(_outproj_kernel, rows_as_tiles=rows_as_tiles),
        grid=(t // tm,),
        in_specs=[
            pl.BlockSpec((tm, D_LRU), lambda i: (i, 0)),
            pl.BlockSpec((tm, D_DN), lambda i: (i, 0)),
            pl.BlockSpec((D_LRU + D_DN, d), lambda i: (0, 0)),
            pl.BlockSpec((tm, d), lambda i: (i, 0)),
            pl.BlockSpec((1, d), lambda i: (0, 0)),
        ],
        out_specs=[pl.BlockSpec((tm, d), lambda i: (i, 0)), h_spec],
        out_shape=[jax.ShapeDtypeStruct((t, d), F32), h_shape],
        compiler_params=_params("parallel"),
        name="out_proj",
    )(y_lru, y_dn, w, x, gain)


def _ffn_kernel(h_ref, x_ref, wg_ref, wu_ref, wd_ref, o_ref):
    @pl.when(pl.program_id(1) == 0)
    def _():
        o_ref[...] = x_ref[...]

    h = h_ref[...]
    g = _dot(h, wg_ref[...])
    u = _dot(h, wu_ref[...])
    o_ref[...] += _dot((jax.nn.silu(g) * u).astype(BF16), wd_ref[...])


def _ffn_dense(h, x, wg, wu, wd, tm, tf):
    t, d = x.shape
    ff = wg.shape[1]
    return pl.pallas_call(
        _ffn_kernel,
        grid=(t // tm, ff // tf),
        in_specs=[
            pl.BlockSpec((tm, d), lambda i, f: (i, 0)),
            pl.BlockSpec((tm, d), lambda i, f: (i, 0)),
            pl.BlockSpec((d, tf), lambda i, f: (0, f)),
            pl.BlockSpec((d, tf), lambda i, f: (0, f)),
            pl.BlockSpec((tf, d), lambda i, f: (f, 0)),
        ],
        out_specs=pl.BlockSpec((tm, d), lambda i, f: (i, 0)),
        out_shape=jax.ShapeDtypeStruct((t, d), F32),
        compiler_params=_params("parallel", "arbitrary"),
        name="ffn_dense",
    )(h, x, wg, wu, wd)


R_E1, R_E2, R_W1, R_W2, R_RANK1, R_RANK2 = range(6)


def _router_kernel(x_ref, g_ref, rt_ref, mr_ref, mc_ref, cnt_ref, carry_ref, *, tm):
    @pl.when(pl.program_id(0) == 0)
    def _():
        carry_ref[...] = jnp.zeros_like(carry_ref)

    h = _rms(x_ref[...], g_ref[...])
    logits = _dot_nt(rt_ref[...], h, precision=lax.Precision.HIGHEST)
    eidx = lax.broadcasted_iota(I32, (N_EXPERTS, tm), 0)
    m1 = jnp.max(logits, axis=0, keepdims=True)
    i1 = jnp.min(jnp.where(logits == m1, eidx, N_EXPERTS), axis=0, keepdims=True)
    rest = jnp.where(eidx == i1, -jnp.inf, logits)
    m2 = jnp.max(rest, axis=0, keepdims=True)
    i2 = jnp.min(jnp.where(rest == m2, eidx, N_EXPERTS), axis=0, keepdims=True)
    e2 = jnp.exp(m2 - m1)
    w1 = 1.0 / (1.0 + e2)
    w2 = e2 / (1.0 + e2)
    oh1 = eidx == i1
    oh2 = eidx == i2
    cnt = jnp.where(oh1 | oh2, 1.0, 0.0)
    ti = lax.broadcasted_iota(I32, (tm, tm), 0)
    tj = lax.broadcasted_iota(I32, (tm, tm), 1)
    upper = jnp.where(ti <= tj, 1.0, 0.0).astype(BF16)
    cum = _dot(cnt.astype(BF16), upper)
    before = cum - cnt + carry_ref[:, 0:1]
    rank1 = jnp.sum(jnp.where(oh1, before, 0.0), axis=0, keepdims=True)
    rank2 = jnp.sum(jnp.where(oh2, before, 0.0), axis=0, keepdims=True)
    total = carry_ref[:, 0:1] + cum[:, tm - 1:tm]
    carry_ref[...] = jnp.broadcast_to(total, carry_ref.shape)
    cnt_ref[...] = jnp.broadcast_to(total, cnt_ref.shape).astype(I32)
    rec = jnp.concatenate(
        [i1.astype(F32), i2.astype(F32), w1, w2, rank1, rank2,
         jnp.zeros((LANES - 6, tm), F32)], axis=0)
    mr_ref[...] = rec[0:SUBLANES, :]
    mc_ref[...] = rec.T


def _router(x, gain, router_t, tm):
    t, d = x.shape
    return pl.pallas_call(
        functools.partial(_router_kernel, tm=tm),
        grid=(t // tm,),
        in_specs=[
            pl.BlockSpec((tm, d), lambda i: (i, 0)),
            pl.BlockSpec((1, d), lambda i: (0, 0)),
            pl.BlockSpec((N_EXPERTS, d), lambda i: (0, 0)),
        ],
        out_specs=[
            pl.BlockSpec((SUBLANES, tm), lambda i: (0, i)),
            pl.BlockSpec((tm, LANES), lambda i: (i, 0)),
            pl.BlockSpec((N_EXPERTS, LANES), lambda i: (0, 0)),
        ],
        out_shape=[
            jax.ShapeDtypeStruct((SUBLANES, t), F32),
            jax.ShapeDtypeStruct((t, LANES), F32),
            jax.ShapeDtypeStruct((N_EXPERTS, LANES), I32),
        ],
        scratch_shapes=[pltpu.VMEM((N_EXPERTS, LANES), F32)],
        compiler_params=_params("arbitrary"),
        name="moe_router",
    )(x, gain, router_t)


DMA_UNROLL = 8


def _dispatch_kernel(d1_ref, d2_ref, zb_ref, h_ref, xs_ref, z_ref, sem, zsem, *, tm, tmb, n_zero):
    base = pl.program_id(0) * tm

    @pl.when(pl.program_id(0) == 0)
    def _():
        z_ref[...] = jnp.zeros_like(z_ref)
        for e in range(n_zero):
            fill = pltpu.make_async_copy(z_ref, xs_ref.at[pl.ds(zb_ref[e], tmb)], zsem)
            fill.start()
            fill.wait()

    def issue(g, carry):
        for u in range(DMA_UNROLL):
            r = g * DMA_UNROLL + u
            pltpu.make_async_copy(h_ref.at[r], xs_ref.at[d1_ref[base + r]], sem.at[0]).start()
            pltpu.make_async_copy(h_ref.at[r], xs_ref.at[d2_ref[base + r]], sem.at[1]).start()
        return carry

    lax.fori_loop(0, tm // DMA_UNROLL, issue, 0)
    pltpu.make_async_copy(h_ref, xs_ref.at[pl.ds(0, tm)], sem.at[0]).wait()
    pltpu.make_async_copy(h_ref, xs_ref.at[pl.ds(0, tm)], sem.at[1]).wait()


def _dispatch(dest1, dest2, zero_blocks, h_tiles, n_slots, tm, tmb):
    t = h_tiles.shape[0]
    return pl.pallas_call(
        functools.partial(_dispatch_kernel, tm=tm, tmb=tmb, n_zero=zero_blocks.shape[0]),
        grid_spec=pltpu.PrefetchScalarGridSpec(
            num_scalar_prefetch=3,
            grid=(t // tm,),
            in_specs=[pl.BlockSpec((tm, ROW_CHUNKS, LANES), lambda i, d1, d2, zb: (i, 0, 0))],
            out_specs=pl.BlockSpec(memory_space=pl.ANY),
            scratch_shapes=[
                pltpu.VMEM((tmb, ROW_CHUNKS, LANES), F32),
                pltpu.SemaphoreType.DMA((2,)),
                pltpu.SemaphoreType.DMA(()),
            ],
        ),
        out_shape=jax.ShapeDtypeStruct((n_slots, ROW_CHUNKS, LANES), F32),
        compiler_params=_params("arbitrary"),
        name="moe_dispatch",
    )(dest1, dest2, zero_blocks, h_tiles)


def _gmm_kernel(be_ref, nu_ref, xs_ref, wg_ref, wu_ref, wd_ref, y_ref, xb_ref, acc_ref):
    del be_ref
    f = pl.program_id(1)
    used = pl.program_id(0) < nu_ref[0]

    @pl.when(jnp.logical_and(jnp.logical_not(used), f == pl.num_programs(1) - 1))
    def _():
        y_ref[...] = jnp.zeros_like(y_ref)

    @pl.when(used)
    def _():
        @pl.when(f == 0)
        def _():
            xb_ref[...] = jnp.concatenate(
                [xs_ref[:, cc, :] for cc in range(ROW_CHUNKS)], axis=1).astype(BF16)
            acc_ref[...] = jnp.zeros_like(acc_ref)

        xb = xb_ref[...]
        g = _dot(xb, wg_ref[...])
        u = _dot(xb, wu_ref[...])
        acc_ref[...] += _dot((jax.nn.silu(g) * u).astype(BF16), wd_ref[...])

        @pl.when(f == pl.num_programs(1) - 1)
        def _():
            for cc in range(ROW_CHUNKS):
                y_ref[:, cc, :] = acc_ref[:, cc * LANES:(cc + 1) * LANES]


def _gmm(block_expert, n_used, xs, wg, wu, wd, tm, tf):
    n_slots = xs.shape[0]
    d = D_MODEL
    ff = wg.shape[2]
    nf = ff // tf

    def blk(b, nu):
        return jnp.minimum(b, nu[0] - 1)

    def fidx(b, f, nu):
        return jnp.where(b < nu[0], f, nf - 1)

    return pl.pallas_call(
        _gmm_kernel,
        grid_spec=pltpu.PrefetchScalarGridSpec(
            num_scalar_prefetch=2,
            grid=(n_slots // tm, nf),
            in_specs=[
                pl.BlockSpec((tm, ROW_CHUNKS, LANES), lambda b, f, be, nu: (blk(b, nu), 0, 0)),
                pl.BlockSpec((None, d, tf), lambda b, f, be, nu: (be[blk(b, nu)], 0, fidx(b, f, nu))),
                pl.BlockSpec((None, d, tf), lambda b, f, be, nu: (be[blk(b, nu)], 0, fidx(b, f, nu))),
                pl.BlockSpec((None, tf, d), lambda b, f, be, nu: (be[blk(b, nu)], fidx(b, f, nu), 0)),
            ],
            out_specs=pl.BlockSpec((tm, ROW_CHUNKS, LANES), lambda b, f, be, nu: (b, 0, 0)),
            scratch_shapes=[pltpu.VMEM((tm, d), BF16), pltpu.VMEM((tm, d), F32)],
        ),
        out_shape=jax.ShapeDtypeStruct((n_slots, ROW_CHUNKS, LANES), F32),
        compiler_params=_params("arbitrary", "arbitrary"),
        name="moe_gmm",
    )(block_expert, n_used, xs, wg, wu, wd)


def _combine_kernel(d1_ref, d2_ref, x_ref, mc_ref, y_ref, g_ref, o_ref, b1_ref, b2_ref, sem, *, tm):
    i = pl.program_id(0)
    slot = i % 2

    def gather(step, sl):
        base = step * tm

        def issue(g, carry):
            for u in range(DMA_UNROLL):
                r = g * DMA_UNROLL + u
                pltpu.make_async_copy(y_ref.at[d1_ref[base + r]], b1_ref.at[sl, r], sem.at[sl, 0]).start()
                pltpu.make_async_copy(y_ref.at[d2_ref[base + r]], b2_ref.at[sl, r], sem.at[sl, 1]).start()
            return carry

        lax.fori_loop(0, tm // DMA_UNROLL, issue, 0)

    @pl.when(i == 0)
    def _():
        gather(0, 0)

    @pl.when(i + 1 < pl.num_programs(0))
    def _():
        gather(i + 1, 1 - slot)

    pltpu.make_async_copy(y_ref.at[pl.ds(0, tm)], b1_ref.at[slot], sem.at[slot, 0]).wait()
    pltpu.make_async_copy(y_ref.at[pl.ds(0, tm)], b2_ref.at[slot], sem.at[slot, 1]).wait()
    w1 = mc_ref[:, R_W1:R_W1 + 1]
    w2 = mc_ref[:, R_W2:R_W2 + 1]
    xn = jnp.concatenate(
        [x_ref[:, cc * LANES:(cc + 1) * LANES]
         + (w1 * b1_ref[slot, :, cc, :] + w2 * b2_ref[slot, :, cc, :])
         for cc in range(ROW_CHUNKS)], axis=1)
    o_ref[...] = _rms(xn, g_ref[...])


def _combine(dest1, dest2, x, meta_col, y, gain, tm):
    t, d = x.shape
    return pl.pallas_call(
        functools.partial(_combine_kernel, tm=tm),
        grid_spec=pltpu.PrefetchScalarGridSpec(
            num_scalar_prefetch=2,
            grid=(t // tm,),
            in_specs=[
                pl.BlockSpec((tm, d), lambda i, d1, d2: (i, 0)),
                pl.BlockSpec((tm, LANES), lambda i, d1, d2: (i, 0)),
                pl.BlockSpec(memory_space=pl.ANY),
                pl.BlockSpec((1, d), lambda i, d1, d2: (0, 0)),
            ],
            out_specs=pl.BlockSpec((tm, d), lambda i, d1, d2: (i, 0)),
            scratch_shapes=[
                pltpu.VMEM((2, tm, ROW_CHUNKS, LANES), F32),
                pltpu.VMEM((2, tm, ROW_CHUNKS, LANES), F32),
                pltpu.SemaphoreType.DMA((2, 2)),
            ],
        ),
        out_shape=jax.ShapeDtypeStruct((t, d), F32),
        compiler_params=_params("arbitrary"),
        name="moe_combine",
    )(dest1, dest2, x, meta_col, y, gain)


def _tiles(t, seq):
    pick = lambda want, total: want if total % want == 0 else total
    return dict(
        proj_tm=pick(1024, t), proj_tn=1024,
        lru_tm=pick(256, seq),
        gdn_tm=pick(256, seq), gdn_hb=4,
        out_tm=pick(512, t),
        ffn_tm=pick(512, t), ffn_tf=512,
        router_tm=pick(512, t),
        dispatch_tm=pick(256, t),
        gmm_tm=512, gmm_tf=512,
        combine_tm=pick(256, t),
    )


def _moe(x, h_tiles, norm_ffn, router, wg, wu, wd, norm_final, tl):
    t = x.shape[0]
    tmb = tl["gmm_tm"]
    meta_row, meta_col, counts = _router(x, norm_ffn, router.T, tl["router_tm"])
    counts = counts[:, 0]
    padded = (counts + tmb - 1) // tmb * tmb
    pad_end = jnp.cumsum(padded)
    pad_start = pad_end - padded
    e1 = meta_row[R_E1].astype(I32)
    e2 = meta_row[R_E2].astype(I32)
    experts = jnp.arange(N_EXPERTS, dtype=I32)[:, None]
    start1 = jnp.sum(jnp.where(e1[None, :] == experts, pad_start[:, None], 0), axis=0)
    start2 = jnp.sum(jnp.where(e2[None, :] == experts, pad_start[:, None], 0), axis=0)
    dest1 = start1 + meta_row[R_RANK1].astype(I32)
    dest2 = start2 + meta_row[R_RANK2].astype(I32)
    n_slots = 2 * t + N_EXPERTS * tmb
    n_blocks = n_slots // tmb
    block_start = jnp.arange(n_blocks, dtype=I32) * tmb
    block_expert = jnp.minimum(
        jnp.sum((block_start[:, None] >= pad_end[None, :]).astype(I32), axis=1), N_EXPERTS - 1)
    n_used = (pad_end[-1] // tmb).astype(I32).reshape(1)
    zero_blocks = jnp.concatenate([
        jnp.maximum(pad_end - tmb, 0),
        jnp.minimum(pad_end[-1] + jnp.arange(N_EXPERTS, dtype=I32) * tmb, n_slots - tmb),
    ]).astype(I32)
    xs = _dispatch(dest1, dest2, zero_blocks, h_tiles, n_slots, tl["dispatch_tm"], tmb)
    y = _gmm(block_expert, n_used, xs, wg, wu, wd, tmb, tl["gmm_tf"])
    return _combine(dest1, dest2, x, meta_col, y, norm_final, tl["combine_tm"])


def kernel(x, norm_mix, w_in, conv_lru_w, conv_lru_b, lru_w_r, lru_b_r, lru_w_i, lru_b_i,
           lru_lambda, lru_out_norm, conv_qkv_w, dn_a_log, dn_dt_bias, dn_out_norm, w_out,
           norm_ffn, ffn_w_gate, ffn_w_up, ffn_w_down, moe_router, moe_w_gate, moe_w_up,
           moe_w_down, norm_final):
    bsz, seq, d = x.shape
    t = bsz * seq
    depth = w_in.shape[0]
    tl = _tiles(t, seq)
    row = lambda v: v.reshape(1, -1)
    xt = x.reshape(t, d)
    out = None
    for l in range(depth):
        is_moe = l % 2 == 1
        w_main = w_in[l, :, :D_PROJ].astype(BF16)
        w_small = w_in[l, :, D_PROJ:]
        w_small_p = jnp.pad(w_small, ((0, 0), (0, LANES - 2 * DN_HEADS))).astype(BF16)
        proj, ba_col, ba_row = _in_proj(xt, row(norm_mix[l]), w_main, w_small_p,
                                        w_small.T.astype(BF16), tl["proj_tm"], tl["proj_tn"])
        y_lru = _rg_lru(proj, conv_lru_w[l], row(conv_lru_b[l]), lru_w_r[l].astype(BF16),
                        row(lru_b_r[l]), lru_w_i[l].astype(BF16), row(lru_b_i[l]),
                        row(lru_lambda[l]), row(lru_out_norm[l]), bsz, tl["lru_tm"])
        y_dn = _gated_deltanet(proj, ba_col, ba_row, conv_qkv_w[l], dn_a_log[l], dn_dt_bias[l],
                               row(dn_out_norm[l]), bsz, tl["gdn_tm"], tl["gdn_hb"])
        xt, h2 = _out_proj(y_lru, y_dn, w_out[l].astype(BF16), xt, row(norm_ffn[l]),
                           tl["out_tm"], rows_as_tiles=is_moe)
        j = l // 2
        if not is_moe:
            xt = _ffn_dense(h2, xt, ffn_w_gate[j].astype(BF16), ffn_w_up[j].astype(BF16),
                            ffn_w_down[j].astype(BF16), tl["ffn_tm"], tl["ffn_tf"])
        else:
            assert l == depth - 1, "the routed layer fuses the final RMSNorm"
            out = _moe(xt, h2, row(norm_ffn[l]), moe_router[j], moe_w_gate[j].astype(BF16),
                       moe_w_up[j].astype(BF16), moe_w_down[j].astype(BF16), row(norm_final), tl)
    return out.reshape(bsz, seq, d)
```

```python
import functools

import jax
import jax.numpy as jnp
from jax import lax
from jax.experimental import pallas as pl
from jax.experimental.pallas import tpu as pltpu

F32 = jnp.float32
BF16 = jnp.bfloat16
I32 = jnp.int32

D_MODEL = 2048
D_LRU = 1024
LRU_BLOCKS = 8
LRU_BLOCK_W = 128
LRU_C = 8.0
CONV_W = 4
D_DN = 1024
DN_HEAD_DIM = 128
DN_HEADS = 8
DN_CHUNK = 64
D_PROJ = 2 * D_LRU + 4 * D_DN
N_EXPERTS = 8
EPS = 1e-6

LANES = 128
SUBLANES = 8
ROW_CHUNKS = D_MODEL // LANES
VMEM_LIMIT = 56 * 1024 * 1024


def _params(*sem):
    return pltpu.CompilerParams(dimension_semantics=sem, vmem_limit_bytes=VMEM_LIMIT)


def _rms(x, gain):
    ms = jnp.mean(x * x, axis=-1, keepdims=True)
    return x * lax.rsqrt(ms + EPS) * gain


def _softplus(x):
    return jnp.maximum(x, 0.0) + jnp.log1p(jnp.exp(-jnp.abs(x)))


def _dot(a, b):
    return jnp.dot(a, b, preferred_element_type=F32)


def _dot_nt(a, b, precision=None):
    return lax.dot_general(a, b, (((1,), (1,)), ((), ())),
                           preferred_element_type=F32, precision=precision)


def _dot_tn(a, b):
    return lax.dot_general(a, b, (((0,), (0,)), ((), ())), preferred_element_type=F32)


def _inproj_kernel(x_ref, g_ref, w_ref, ws_ref, wst_ref, o_ref, oc_ref, or_ref, h_ref):
    @pl.when(pl.program_id(1) == 0)
    def _():
        h = _rms(x_ref[...], g_ref[...]).astype(BF16)
        h_ref[...] = h
        oc_ref[...] = _dot(h, ws_ref[...])
        or_ref[...] = _dot_nt(wst_ref[...], h)

    o_ref[...] = _dot(h_ref[...], w_ref[...].astype(BF16))


def _in_proj(x, gain, w_in, layer, w_small, w_small_t, tm, tn):
    t, d = x.shape
    n = D_PROJ
    return pl.pallas_call(
        _inproj_kernel,
        grid=(t // tm, n // tn),
        in_specs=[
            pl.BlockSpec((tm, d), lambda i, j: (i, 0)),
            pl.BlockSpec((1, d), lambda i, j: (0, 0)),
            pl.BlockSpec((None, d, tn), lambda i, j: (layer, 0, j)),
            pl.BlockSpec((d, LANES), lambda i, j: (0, 0)),
            pl.BlockSpec((2 * DN_HEADS, d), lambda i, j: (0, 0)),
        ],
        out_specs=[
            pl.BlockSpec((tm, tn), lambda i, j: (i, j)),
            pl.BlockSpec((tm, LANES), lambda i, j: (i, 0)),
            pl.BlockSpec((2 * DN_HEADS, tm), lambda i, j: (0, i)),
        ],
        out_shape=[
            jax.ShapeDtypeStruct((t, n), F32),
            jax.ShapeDtypeStruct((t, LANES), F32),
            jax.ShapeDtypeStruct((2 * DN_HEADS, t), F32),
        ],
        scratch_shapes=[pltpu.VMEM((tm, d), BF16)],
        compiler_params=_params("parallel", "arbitrary"),
        name="in_proj",
    )(x, gain, w_in, w_small, w_small_t)


def _lru_kernel(x_ref, gate_ref, cw_ref, cb_ref, wr_ref, br_ref, wi_ref, bi_ref, lam_ref,
                on_ref, o_ref, xs_ref, hc_ref, y_ref, *, tm):
    @pl.when(pl.program_id(1) == 0)
    def _():
        xs_ref[0:SUBLANES, :] = jnp.zeros((SUBLANES, D_LRU), F32)
        hc_ref[...] = jnp.zeros_like(hc_ref)

    xs_ref[SUBLANES:SUBLANES + tm, :] = x_ref[...]
    row = lax.broadcasted_iota(I32, (tm, LRU_BLOCK_W), 0)
    ssq = jnp.zeros((tm, 1), F32)
    for n in range(LRU_BLOCKS):
        sl = slice(n * LRU_BLOCK_W, (n + 1) * LRU_BLOCK_W)
        x_ext = xs_ref[:, sl]
        xc = pltpu.roll(x_ext, CONV_W - 1, 0)[SUBLANES:, :] * cw_ref[0:1, sl]
        for k in range(1, CONV_W - 1):
            xc = xc + pltpu.roll(x_ext, CONV_W - 1 - k, 0)[SUBLANES:, :] * cw_ref[k:k + 1, sl]
        xc = xc + x_ext[SUBLANES:, :] * cw_ref[CONV_W - 1:CONV_W, sl]
        xc = xc + cb_ref[:, sl]
        xb = xc.astype(BF16)
        r = jax.nn.sigmoid(_dot(xb, wr_ref[n]) + br_ref[:, sl])
        i = jax.nn.sigmoid(_dot(xb, wi_ref[n]) + bi_ref[:, sl])
        log_a = -LRU_C * r * _softplus(-lam_ref[:, sl])
        a = jnp.exp(log_a)
        b = jnp.sqrt(-jnp.tanh(log_a) * (1.0 + a * a)) * (i * xc)
        d = 1
        while d < tm:
            keep = row >= d
            b = jnp.where(keep, a * pltpu.roll(b, d, 0) + b, b)
            a = jnp.where(keep, a * pltpu.roll(a, d, 0), a)
            d *= 2
        h = b + a * hc_ref[0:1, sl]
        hc_ref[0:1, sl] = h[tm - 1:tm, :]
        y = h * jax.nn.gelu(gate_ref[:, sl], approximate=True)
        y_ref[:, sl] = y
        ssq = ssq + jnp.sum(y * y, axis=-1, keepdims=True)
    inv = lax.rsqrt(ssq / D_LRU + EPS)
    o_ref[...] = (y_ref[...] * inv * on_ref[...]).astype(BF16)
    xs_ref[0:SUBLANES, :] = x_ref[tm - SUBLANES:tm, :]


def _rg_lru(proj, conv_w, conv_b, w_r, b_r, w_i, b_i, lam, out_norm, bsz, tm):
    t = proj.shape[0]
    ns = t // bsz // tm
    row = lambda b, s: (0, 0)
    return pl.pallas_call(
        functools.partial(_lru_kernel, tm=tm),
        grid=(bsz, ns),
        in_specs=[
            pl.BlockSpec((tm, D_LRU), lambda b, s: (b * ns + s, 0)),
            pl.BlockSpec((tm, D_LRU), lambda b, s: (b * ns + s, 1)),
            pl.BlockSpec((CONV_W, D_LRU), row),
            pl.BlockSpec((1, D_LRU), row),
            pl.BlockSpec((LRU_BLOCKS, LRU_BLOCK_W, LRU_BLOCK_W), lambda b, s: (0, 0, 0)),
            pl.BlockSpec((1, D_LRU), row),
            pl.BlockSpec((LRU_BLOCKS, LRU_BLOCK_W, LRU_BLOCK_W), lambda b, s: (0, 0, 0)),
            pl.BlockSpec((1, D_LRU), row),
            pl.BlockSpec((1, D_LRU), row),
            pl.BlockSpec((1, D_LRU), row),
        ],
        out_specs=pl.BlockSpec((tm, D_LRU), lambda b, s: (b * ns + s, 0)),
        out_shape=jax.ShapeDtypeStruct((t, D_LRU), BF16),
        scratch_shapes=[
            pltpu.VMEM((tm + SUBLANES, D_LRU), F32),
            pltpu.VMEM((SUBLANES, D_LRU), F32),
            pltpu.VMEM((tm, D_LRU), F32),
        ],
        compiler_params=_params("parallel", "arbitrary"),
        name="rg_lru",
    )(proj, proj, conv_w, conv_b, w_r, b_r, w_i, b_i, lam, out_norm)


def _gdn_kernel(q_ref, k_ref, v_ref, gate_ref, cwq_ref, cwk_ref, cwv_ref, bac_ref, bar_ref,
                alc_ref, dtc_ref, alr_ref, dtr_ref, on_ref, o_ref,
                qs_ref, ks_ref, vs_ref, st_ref, gr_ref, *, tm, hb):
    hg = pl.program_id(1)
    c = DN_CHUNK

    @pl.when(pl.program_id(2) == 0)
    def _():
        zeros = jnp.zeros((SUBLANES, hb * DN_HEAD_DIM), F32)
        qs_ref[0:SUBLANES, :] = zeros
        ks_ref[0:SUBLANES, :] = zeros
        vs_ref[0:SUBLANES, :] = zeros
        st_ref[...] = jnp.zeros_like(st_ref)

    def conv_silu(x_ref, xs_ref, cw_ref):
        xs_ref[SUBLANES:SUBLANES + tm, :] = x_ref[...]
        x_ext = xs_ref[...]
        y = pltpu.roll(x_ext, CONV_W - 1, 0)[SUBLANES:, :] * cw_ref[0:1, :]
        for kk in range(1, CONV_W - 1):
            y = y + pltpu.roll(x_ext, CONV_W - 1 - kk, 0)[SUBLANES:, :] * cw_ref[kk:kk + 1, :]
        y = y + x_ext[SUBLANES:, :] * cw_ref[CONV_W - 1:CONV_W, :]
        xs_ref[0:SUBLANES, :] = x_ref[tm - SUBLANES:tm, :]
        return jax.nn.silu(y)

    q_all = conv_silu(q_ref, qs_ref, cwq_ref)
    k_all = conv_silu(k_ref, ks_ref, cwk_ref)
    v_all = conv_silu(v_ref, vs_ref, cwv_ref)

    bac = bac_ref[...]
    lane = lax.broadcasted_iota(I32, (tm, LANES), 1)
    rowc = lax.broadcasted_iota(I32, (tm, LANES), 0) % c
    g_all = -jnp.exp(alc_ref[...]) * _softplus(bac + dtc_ref[...])
    d = 1
    while d < c:
        g_all = g_all + jnp.where(rowc >= d, pltpu.roll(g_all, d, 0), 0.0)
        d *= 2
    sig_all = jax.nn.sigmoid(bac)

    g_row = -jnp.exp(alr_ref[...]) * _softplus(bar_ref[DN_HEADS:2 * DN_HEADS, :] + dtr_ref[...])
    lanec = lax.broadcasted_iota(I32, (DN_HEADS, tm), 1) % c
    d = 1
    while d < c:
        g_row = g_row + jnp.where(lanec >= d, pltpu.roll(g_row, d, 1), 0.0)
        d *= 2
    gr_ref[...] = g_row

    ri = lax.broadcasted_iota(I32, (c, c), 0)
    ci = lax.broadcasted_iota(I32, (c, c), 1)
    causal = ri >= ci
    strict = ri > ci
    eye = (ri == ci).astype(F32)
    on = on_ref[...]
    n_ch = tm // c

    probs = []
    for j in range(hb):
        hd = hg * hb + j
        cs = slice(j * DN_HEAD_DIM, (j + 1) * DN_HEAD_DIM)
        q = q_all[:, cs]
        k = k_all[:, cs]
        q = q * lax.rsqrt(jnp.sum(q * q, axis=-1, keepdims=True) + EPS) * (DN_HEAD_DIM ** -0.5)
        k = k * lax.rsqrt(jnp.sum(k * k, axis=-1, keepdims=True) + EPS)
        gcum = jnp.sum(jnp.where(lane == hd + DN_HEADS, g_all, 0.0), axis=1, keepdims=True)
        beta = jnp.sum(jnp.where(lane == hd, sig_all, 0.0), axis=1, keepdims=True)
        grow_all = gr_ref[pl.ds(hd, 1), :]
        for ch in range(n_ch):
            rs = slice(ch * c, (ch + 1) * c)
            gcol = gcum[rs]
            kc = k[rs]
            kb = kc * beta[rs]
            probs.append(dict(
                j=j, ch=ch, qc=q[rs], kc=kc, kb=kb, vb=v_all[rs, cs] * beta[rs], gcol=gcol,
                eg=jnp.exp(gcol), glast=gcol[c - 1:c, :],
                decay=jnp.where(causal, jnp.exp(jnp.where(causal, gcol - grow_all[:, rs], 0.0)), 0.0)))
    for p in probs:
        p["prod"] = _dot_nt(jnp.concatenate([p["qc"], p["kb"]], axis=0).astype(BF16),
                            p["kc"].astype(BF16))
    for p in probs:
        p["qk"] = jnp.where(causal, p["prod"][0:c] * p["decay"], 0.0).astype(BF16)
        low = jnp.where(strict, p["prod"][c:2 * c] * p["decay"], 0.0)
        p["inv"] = eye - low
        p["pw"] = low.astype(BF16)
    n_sq = 1
    while 2 * n_sq < c:
        for p in probs:
            p["pw"] = _dot(p["pw"], p["pw"]).astype(BF16)
        for p in probs:
            p["inv"] = p["inv"] + _dot(p["inv"].astype(BF16), p["pw"])
        n_sq *= 2
    for p in probs:
        rhs = jnp.concatenate([p["vb"], p["kb"] * p["eg"]], axis=1).astype(BF16)
        sol = _dot(p["inv"].astype(BF16), rhs)
        p["u"] = sol[:, 0:DN_HEAD_DIM]
        w = sol[:, DN_HEAD_DIM:2 * DN_HEAD_DIM]
        p["wq"] = jnp.concatenate([w, p["qc"] * p["eg"]], axis=0).astype(BF16)
        p["k_dec"] = (p["kc"] * jnp.exp(p["glast"] - p["gcol"])).astype(BF16)
        p["g_end"] = jnp.exp(p["glast"])

    states = [st_ref[j] for j in range(hb)]
    for ch in range(n_ch):
        rs = slice(ch * c, (ch + 1) * c)
        cur = [probs[j * n_ch + ch] for j in range(hb)]
        wss = [_dot(p["wq"], states[j].astype(BF16)) for j, p in enumerate(cur)]
        v_nbs = [(p["u"] - ws[0:c]).astype(BF16) for p, ws in zip(cur, wss)]
        outs = [ws[c:2 * c] + _dot(p["qk"], v_nb) for p, ws, v_nb in zip(cur, wss, v_nbs)]
        states = [states[j] * p["g_end"] + _dot_tn(p["k_dec"], v_nb)
                  for j, (p, v_nb) in enumerate(zip(cur, v_nbs))]
        for j, o in enumerate(outs):
            cs = slice(j * DN_HEAD_DIM, (j + 1) * DN_HEAD_DIM)
            o_ref[rs, cs] = (_rms(o, on) * jax.nn.silu(gate_ref[rs, cs])).astype(BF16)
    for j in range(hb):
        st_ref[j] = states[j]


def _gated_deltanet(proj, ba_col, ba_row, conv_w, a_log, dt_bias, out_norm, bsz, tm, hb):
    t = proj.shape[0]
    ns = t // bsz // tm
    hh = DN_HEADS
    ng = hh // hb
    wb = hb * DN_HEAD_DIM
    q0 = 2 * D_LRU // wb
    pad = jnp.zeros((hh,), F32)
    al_c = jnp.concatenate([pad, a_log, jnp.zeros((LANES - 2 * hh,), F32)]).reshape(1, LANES)
    dt_c = jnp.concatenate([pad, dt_bias, jnp.zeros((LANES - 2 * hh,), F32)]).reshape(1, LANES)
    al_r = a_log.reshape(hh, 1)
    dt_r = dt_bias.reshape(hh, 1)
    tok = lambda col0: (lambda b, h, s: (b * ns + s, col0 + h))
    cw = lambda col0: (lambda b, h, s: (0, col0 + h))
    const = lambda b, h, s: (0, 0)
    return pl.pallas_call(
        functools.partial(_gdn_kernel, tm=tm, hb=hb),
        grid=(bsz, ng, ns),
        in_specs=[
            pl.BlockSpec((tm, wb), tok(q0)),
            pl.BlockSpec((tm, wb), tok(q0 + ng)),
            pl.BlockSpec((tm, wb), tok(q0 + 2 * ng)),
            pl.BlockSpec((tm, wb), tok(q0 + 3 * ng)),
            pl.BlockSpec((CONV_W, wb), cw(0)),
            pl.BlockSpec((CONV_W, wb), cw(ng)),
            pl.BlockSpec((CONV_W, wb), cw(2 * ng)),
            pl.BlockSpec((tm, LANES), lambda b, h, s: (b * ns + s, 0)),
            pl.BlockSpec((2 * hh, tm), lambda b, h, s: (0, b * ns + s)),
            pl.BlockSpec((1, LANES), const),
            pl.BlockSpec((1, LANES), const),
            pl.BlockSpec((hh, 1), const),
            pl.BlockSpec((hh, 1), const),
            pl.BlockSpec((1, DN_HEAD_DIM), const),
        ],
        out_specs=pl.BlockSpec((tm, wb), lambda b, h, s: (b * ns + s, h)),
        out_shape=jax.ShapeDtypeStruct((t, D_DN), BF16),
        scratch_shapes=[
            pltpu.VMEM((tm + SUBLANES, wb), F32),
            pltpu.VMEM((tm + SUBLANES, wb), F32),
            pltpu.VMEM((tm + SUBLANES, wb), F32),
            pltpu.VMEM((hb, DN_HEAD_DIM, DN_HEAD_DIM), F32),
            pltpu.VMEM((hh, tm), F32),
        ],
        compiler_params=_params("parallel", "parallel", "arbitrary"),
        name="gated_deltanet",
    )(proj, proj, proj, proj, conv_w, conv_w, conv_w, ba_col, ba_row,
      al_c, dt_c, al_r, dt_r, out_norm)


def _outproj_kernel(yl_ref, yd_ref, w_ref, x_ref, g_ref, xo_ref, h_ref, *, rows_as_tiles):
    acc = _dot(yl_ref[...], w_ref[0:D_LRU, :]) + _dot(yd_ref[...], w_ref[D_LRU:D_LRU + D_DN, :])
    xn = x_ref[...] + acc
    xo_ref[...] = xn
    hn = _rms(xn, g_ref[...])
    if rows_as_tiles:
        for cc in range(ROW_CHUNKS):
            h_ref[:, cc, :] = hn[:, cc * LANES:(cc + 1) * LANES]
    else:
        h_ref[...] = hn.astype(h_ref.dtype)


def _out_proj(y_lru, y_dn, w, x, gain, tm, rows_as_tiles):
    t, d = x.shape
    if rows_as_tiles:
        h_spec = pl.BlockSpec((tm, ROW_CHUNKS, LANES), lambda i: (i, 0, 0))
        h_shape = jax.ShapeDtypeStruct((t, ROW_CHUNKS, LANES), F32)
    else:
        h_spec = pl.BlockSpec((tm, d), lambda i: (i, 0))
        h_shape = jax.ShapeDtypeStruct((t, d), BF16)
    return pl.pallas_call(
        functools.partial(_outproj_kernel, rows_as_tiles=rows_as_tiles),
        grid=(t // tm,),
        in_specs=[
            pl.BlockSpec((tm, D_LRU), lambda i: (i, 0)),
            pl.BlockSpec((tm, D_DN), lambda i: (i, 0)),
            pl.BlockSpec((D_LRU + D_DN, d), lambda i: (0, 0)),
            pl.BlockSpec((tm, d), lambda i: (i, 0)),
            pl.BlockSpec((1, d), lambda i: (0, 0)),
        ],
        out_specs=[pl.BlockSpec((tm, d), lambda i: (i, 0)), h_spec],
        out_shape=[jax.ShapeDtypeStruct((t, d), F32), h_shape],
        compiler_params=_params("parallel"),
        name="out_proj",
    )(y_lru, y_dn, w, x, gain)


CAST_ROWS = 128


def _ffn_kernel(h_ref, x_ref, wg_ref, wu_ref, wd_ref, *rest, cast_blocks):
    n_cast = len(cast_blocks)
    src_refs, o_ref, dst_refs = rest[:n_cast], rest[n_cast], rest[n_cast + 1:]

    @pl.when(pl.program_id(1) == 0)
    def _():
        o_ref[...] = x_ref[...]

    h = h_ref[...]
    g = _dot(h, wg_ref[...])
    u = _dot(h, wu_ref[...])
    o_ref[...] += _dot((jax.nn.silu(g) * u).astype(BF16), wd_ref[...])

    step = pl.program_id(0) * pl.num_programs(1) + pl.program_id(1)
    for src, dst, nb in zip(src_refs, dst_refs, cast_blocks):
        @pl.when(step < nb)
        def _(src=src, dst=dst):
            dst[...] = src[...].astype(BF16)


def _ffn_dense(h, x, wg, wu, wd, tm, tf, to_cast=()):
    t, d = x.shape
    ff = wg.shape[1]
    nf = ff // tf
    cast_blocks = tuple(a.shape[0] // CAST_ROWS for a in to_cast)
    assert all(nb <= (t // tm) * nf for nb in cast_blocks)
    cast_spec = lambda a, nb: pl.BlockSpec(
        (CAST_ROWS, a.shape[1]), lambda i, f: (jnp.minimum(i * nf + f, nb - 1), 0))
    cast_specs = [cast_spec(a, nb) for a, nb in zip(to_cast, cast_blocks)]
    res = pl.pallas_call(
        functools.partial(_ffn_kernel, cast_blocks=cast_blocks),
        grid=(t // tm, nf),
        in_specs=[
            pl.BlockSpec((tm, d), lambda i, f: (i, 0)),
            pl.BlockSpec((tm, d), lambda i, f: (i, 0)),
            pl.BlockSpec((d, tf), lambda i, f: (0, f)),
            pl.BlockSpec((d, tf), lambda i, f: (0, f)),
            pl.BlockSpec((tf, d), lambda i, f: (f, 0)),
        ] + cast_specs,
        out_specs=[pl.BlockSpec((tm, d), lambda i, f: (i, 0))] + cast_specs,
        out_shape=[jax.ShapeDtypeStruct((t, d), F32)]
        + [jax.ShapeDtypeStruct(a.shape, BF16) for a in to_cast],
        compiler_params=_params("arbitrary", "arbitrary"),
        name="ffn_dense",
    )(h, x, wg, wu, wd, *to_cast)
    return res[0], tuple(res[1:])


R_E1, R_E2, R_W1, R_W2, R_RANK1, R_RANK2 = range(6)


def _router_kernel(x_ref, g_ref, rt_ref, mr_ref, mc_ref, cnt_ref, carry_ref, *, tm):
    @pl.when(pl.program_id(0) == 0)
    def _():
        carry_ref[...] = jnp.zeros_like(carry_ref)

    h = _rms(x_ref[...], g_ref[...])
    logits = _dot_nt(rt_ref[...], h, precision=lax.Precision.HIGHEST)
    eidx = lax.broadcasted_iota(I32, (N_EXPERTS, tm), 0)
    m1 = jnp.max(logits, axis=0, keepdims=True)
    i1 = jnp.min(jnp.where(logits == m1, eidx, N_EXPERTS), axis=0, keepdims=True)
    rest = jnp.where(eidx == i1, -jnp.inf, logits)
    m2 = jnp.max(rest, axis=0, keepdims=True)
    i2 = jnp.min(jnp.where(rest == m2, eidx, N_EXPERTS), axis=0, keepdims=True)
    e2 = jnp.exp(m2 - m1)
    w1 = 1.0 / (1.0 + e2)
    w2 = e2 / (1.0 + e2)
    oh1 = eidx == i1
    oh2 = eidx == i2
    cnt = jnp.where(oh1 | oh2, 1.0, 0.0)
    ti = lax.broadcasted_iota(I32, (tm, tm), 0)
    tj = lax.broadcasted_iota(I32, (tm, tm), 1)
    upper = jnp.where(ti <= tj, 1.0, 0.0).astype(BF16)
    cum = _dot(cnt.astype(BF16), upper)
    before = cum - cnt + carry_ref[:, 0:1]
    rank1 = jnp.sum(jnp.where(oh1, before, 0.0), axis=0, keepdims=True)
    rank2 = jnp.sum(jnp.where(oh2, before, 0.0), axis=0, keepdims=True)
    total = carry_ref[:, 0:1] + cum[:, tm - 1:tm]
    carry_ref[...] = jnp.broadcast_to(total, carry_ref.shape)
    cnt_ref[...] = jnp.broadcast_to(total, cnt_ref.shape).astype(I32)
    rec = jnp.concatenate(
        [i1.astype(F32), i2.astype(F32), w1, w2, rank1, rank2,
         jnp.zeros((LANES - 6, tm), F32)], axis=0)
    mr_ref[...] = rec[0:SUBLANES, :]
    mc_ref[...] = rec.T


def _router(x, gain, router_t, tm):
    t, d = x.shape
    return pl.pallas_call(
        functools.partial(_router_kernel, tm=tm),
        grid=(t // tm,),
        in_specs=[
            pl.BlockSpec((tm, d), lambda i: (i, 0)),
            pl.BlockSpec((1, d), lambda i: (0, 0)),
            pl.BlockSpec((N_EXPERTS, d), lambda i: (0, 0)),
        ],
        out_specs=[
            pl.BlockSpec((SUBLANES, tm), lambda i: (0, i)),
            pl.BlockSpec((tm, LANES), lambda i: (i, 0)),
            pl.BlockSpec((N_EXPERTS, LANES), lambda i: (0, 0)),
        ],
        out_shape=[
            jax.ShapeDtypeStruct((SUBLANES, t), F32),
            jax.ShapeDtypeStruct((t, LANES), F32),
            jax.ShapeDtypeStruct((N_EXPERTS, LANES), I32),
        ],
        scratch_shapes=[pltpu.VMEM((N_EXPERTS, LANES), F32)],
        compiler_params=_params("arbitrary"),
        name="moe_router",
    )(x, gain, router_t)


DMA_UNROLL = 8


def _dispatch_kernel(d1_ref, d2_ref, zb_ref, h_ref, xs_ref, z_ref, sem, zsem, *, tm, tmb, n_zero):
    base = pl.program_id(0) * tm

    @pl.when(pl.program_id(0) == 0)
    def _():
        z_ref[...] = jnp.zeros_like(z_ref)
        for e in range(n_zero):
            fill = pltpu.make_async_copy(z_ref, xs_ref.at[pl.ds(zb_ref[e], tmb)], zsem)
            fill.start()
            fill.wait()

    def issue(g, carry):
        for u in range(DMA_UNROLL):
            r = g * DMA_UNROLL + u
            pltpu.make_async_copy(h_ref.at[r], xs_ref.at[d1_ref[base + r]], sem.at[0]).start()
            pltpu.make_async_copy(h_ref.at[r], xs_ref.at[d2_ref[base + r]], sem.at[1]).start()
        return carry

    lax.fori_loop(0, tm // DMA_UNROLL, issue, 0)
    pltpu.make_async_copy(h_ref, xs_ref.at[pl.ds(0, tm)], sem.at[0]).wait()
    pltpu.make_async_copy(h_ref, xs_ref.at[pl.ds(0, tm)], sem.at[1]).wait()


def _dispatch(dest1, dest2, zero_blocks, h_tiles, n_slots, tm, tmb):
    t = h_tiles.shape[0]
    return pl.pallas_call(
        functools.partial(_dispatch_kernel, tm=tm, tmb=tmb, n_zero=zero_blocks.shape[0]),
        grid_spec=pltpu.PrefetchScalarGridSpec(
            num_scalar_prefetch=3,
            grid=(t // tm,),
            in_specs=[pl.BlockSpec((tm, ROW_CHUNKS, LANES), lambda i, d1, d2, zb: (i, 0, 0))],
            out_specs=pl.BlockSpec(memory_space=pl.ANY),
            scratch_shapes=[
                pltpu.VMEM((tmb, ROW_CHUNKS, LANES), F32),
                pltpu.SemaphoreType.DMA((2,)),
                pltpu.SemaphoreType.DMA(()),
            ],
        ),
        out_shape=jax.ShapeDtypeStruct((n_slots, ROW_CHUNKS, LANES), F32),
        compiler_params=_params("arbitrary"),
        name="moe_dispatch",
    )(dest1, dest2, zero_blocks, h_tiles)


def _gmm_kernel(be_ref, nu_ref, xs_ref, wg_ref, wu_ref, wd_ref, y_ref, xb_ref, acc_ref):
    del be_ref
    f = pl.program_id(1)
    used = pl.program_id(0) < nu_ref[0]

    @pl.when(jnp.logical_and(jnp.logical_not(used), f == pl.num_programs(1) - 1))
    def _():
        y_ref[...] = jnp.zeros_like(y_ref)

    @pl.when(used)
    def _():
        @pl.when(f == 0)
        def _():
            xb_ref[...] = jnp.concatenate(
                [xs_ref[:, cc, :] for cc in range(ROW_CHUNKS)], axis=1).astype(BF16)
            acc_ref[...] = jnp.zeros_like(acc_ref)

        xb = xb_ref[...]
        g = _dot(xb, wg_ref[...])
        u = _dot(xb, wu_ref[...])
        acc_ref[...] += _dot((jax.nn.silu(g) * u).astype(BF16), wd_ref[...])

        @pl.when(f == pl.num_programs(1) - 1)
        def _():
            for cc in range(ROW_CHUNKS):
                y_ref[:, cc, :] = acc_ref[:, cc * LANES:(cc + 1) * LANES]


def _gmm(block_expert, n_used, xs, wg, wu, wd, tm, tf):
    n_slots = xs.shape[0]
    d = D_MODEL
    ff = wg.shape[2]
    nf = ff // tf

    def blk(b, nu):
        return jnp.minimum(b, nu[0] - 1)

    def fidx(b, f, nu):
        return jnp.where(b < nu[0], f, nf - 1)

    return pl.pallas_call(
        _gmm_kernel,
        grid_spec=pltpu.PrefetchScalarGridSpec(
            num_scalar_prefetch=2,
            grid=(n_slots // tm, nf),
            in_specs=[
                pl.BlockSpec((tm, ROW_CHUNKS, LANES), lambda b, f, be, nu: (blk(b, nu), 0, 0)),
                pl.BlockSpec((None, d, tf), lambda b, f, be, nu: (be[blk(b, nu)], 0, fidx(b, f, nu))),
                pl.BlockSpec((None, d, tf), lambda b, f, be, nu: (be[blk(b, nu)], 0, fidx(b, f, nu))),
                pl.BlockSpec((None, tf, d), lambda b, f, be, nu: (be[blk(b, nu)], fidx(b, f, nu), 0)),
            ],
            out_specs=pl.BlockSpec((tm, ROW_CHUNKS, LANES), lambda b, f, be, nu: (b, 0, 0)),
            scratch_shapes=[pltpu.VMEM((tm, d), BF16), pltpu.VMEM((tm, d), F32)],
        ),
        out_shape=jax.ShapeDtypeStruct((n_slots, ROW_CHUNKS, LANES), F32),
        compiler_params=_params("arbitrary", "arbitrary"),
        name="moe_gmm",
    )(block_expert, n_used, xs, wg, wu, wd)


def _combine_kernel(d1_ref, d2_ref, x_ref, mc_ref, y_ref, g_ref, o_ref, b1_ref, b2_ref, sem, *, tm):
    i = pl.program_id(0)
    slot = i % 2

    def gather(step, sl):
        base = step * tm

        def issue(g, carry):
            for u in range(DMA_UNROLL):
                r = g * DMA_UNROLL + u
                pltpu.make_async_copy(y_ref.at[d1_ref[base + r]], b1_ref.at[sl, r], sem.at[sl, 0]).start()
                pltpu.make_async_copy(y_ref.at[d2_ref[base + r]], b2_ref.at[sl, r], sem.at[sl, 1]).start()
            return carry

        lax.fori_loop(0, tm // DMA_UNROLL, issue, 0)

    @pl.when(i == 0)
    def _():
        gather(0, 0)

    @pl.when(i + 1 < pl.num_programs(0))
    def _():
        gather(i + 1, 1 - slot)

    pltpu.make_async_copy(y_ref.at[pl.ds(0, tm)], b1_ref.at[slot], sem.at[slot, 0]).wait()
    pltpu.make_async_copy(y_ref.at[pl.ds(0, tm)], b2_ref.at[slot], sem.at[slot, 1]).wait()
    w1 = mc_ref[:, R_W1:R_W1 + 1]
    w2 = mc_ref[:, R_W2:R_W2 + 1]
    xn = jnp.concatenate(
        [x_ref[:, cc * LANES:(cc + 1) * LANES]
         + (w1 * b1_ref[slot, :, cc, :] + w2 * b2_ref[slot, :, cc, :])
         for cc in range(ROW_CHUNKS)], axis=1)
    o_ref[...] = _rms(xn, g_ref[...])


def _combine(dest1, dest2, x, meta_col, y, gain, tm):
    t, d = x.shape
    return pl.pallas_call(
        functools.partial(_combine_kernel, tm=tm),
        grid_spec=pltpu.PrefetchScalarGridSpec(
            num_scalar_prefetch=2,
            grid=(t // tm,),
            in_specs=[
                pl.BlockSpec((tm, d), lambda i, d1, d2: (i, 0)),
                pl.BlockSpec((tm, LANES), lambda i, d1, d2: (i, 0)),
                pl.BlockSpec(memory_space=pl.ANY),
                pl.BlockSpec((1, d), lambda i, d1, d2: (0, 0)),
            ],
            out_specs=pl.BlockSpec((tm, d), lambda i, d1, d2: (i, 0)),
            scratch_shapes=[
                pltpu.VMEM((2, tm, ROW_CHUNKS, LANES), F32),
                pltpu.VMEM((2, tm, ROW_CHUNKS, LANES), F32),
                pltpu.SemaphoreType.DMA((2, 2)),
            ],
        ),
        out_shape=jax.ShapeDtypeStruct((t, d), F32),
        compiler_params=_params("arbitrary"),
        name="moe_combine",
    )(dest1, dest2, x, meta_col, y, gain)


def _tiles(t, seq):
    pick = lambda want, total: want if total % want == 0 else total
    return dict(
        proj_tm=pick(1024, t), proj_tn=1024,
        lru_tm=pick(256, seq),
        gdn_tm=pick(256, seq), gdn_hb=4,
        out_tm=pick(512, t),
        ffn_tm=pick(512, t), ffn_tf=512,
        router_tm=pick(512, t),
        dispatch_tm=pick(256, t),
        gmm_tm=512, gmm_tf=512,
        combine_tm=pick(256, t),
    )


def _moe(x, h_tiles, norm_ffn, router, wg, wu, wd, norm_final, tl):
    t = x.shape[0]
    tmb = tl["gmm_tm"]
    meta_row, meta_col, counts = _router(x, norm_ffn, router.T, tl["router_tm"])
    counts = counts[:, 0]
    padded = (counts + tmb - 1) // tmb * tmb
    pad_end = jnp.cumsum(padded)
    pad_start = pad_end - padded
    e1 = meta_row[R_E1].astype(I32)
    e2 = meta_row[R_E2].astype(I32)
    experts = jnp.arange(N_EXPERTS, dtype=I32)[:, None]
    start1 = jnp.sum(jnp.where(e1[None, :] == experts, pad_start[:, None], 0), axis=0)
    start2 = jnp.sum(jnp.where(e2[None, :] == experts, pad_start[:, None], 0), axis=0)
    dest1 = start1 + meta_row[R_RANK1].astype(I32)
    dest2 = start2 + meta_row[R_RANK2].astype(I32)
    n_slots = 2 * t + N_EXPERTS * tmb
    n_blocks = n_slots // tmb
    block_start = jnp.arange(n_blocks, dtype=I32) * tmb
    block_expert = jnp.minimum(
        jnp.sum((block_start[:, None] >= pad_end[None, :]).astype(I32), axis=1), N_EXPERTS - 1)
    n_used = (pad_end[-1] // tmb).astype(I32).reshape(1)
    zero_blocks = jnp.concatenate([
        jnp.maximum(pad_end - tmb, 0),
        jnp.minimum(pad_end[-1] + jnp.arange(N_EXPERTS, dtype=I32) * tmb, n_slots - tmb),
    ]).astype(I32)
    xs = _dispatch(dest1, dest2, zero_blocks, h_tiles, n_slots, tl["dispatch_tm"], tmb)
    y = _gmm(block_expert, n_used, xs, wg, wu, wd, tmb, tl["gmm_tf"])
    return _combine(dest1, dest2, x, meta_col, y, norm_final, tl["combine_tm"])


def kernel(x, norm_mix, w_in, conv_lru_w, conv_lru_b, lru_w_r, lru_b_r, lru_w_i, lru_b_i,
           lru_lambda, lru_out_norm, conv_qkv_w, dn_a_log, dn_dt_bias, dn_out_norm, w_out,
           norm_ffn, ffn_w_gate, ffn_w_up, ffn_w_down, moe_router, moe_w_gate, moe_w_up,
           moe_w_down, norm_final):
    bsz, seq, d = x.shape
    t = bsz * seq
    depth = w_in.shape[0]
    tl = _tiles(t, seq)
    row = lambda v: v.reshape(1, -1)
    xt = x.reshape(t, d)
    out = None
    moe_bf16 = {}
    for l in range(depth):
        is_moe = l % 2 == 1
        w_small = w_in[l, :, D_PROJ:]
        w_small_p = jnp.pad(w_small, ((0, 0), (0, LANES - 2 * DN_HEADS))).astype(BF16)
        proj, ba_col, ba_row = _in_proj(xt, row(norm_mix[l]), w_in, l, w_small_p,
                                        w_small.T.astype(BF16), tl["proj_tm"], tl["proj_tn"])
        y_lru = _rg_lru(proj, conv_lru_w[l], row(conv_lru_b[l]), lru_w_r[l].astype(BF16),
                        row(lru_b_r[l]), lru_w_i[l].astype(BF16), row(lru_b_i[l]),
                        row(lru_lambda[l]), row(lru_out_norm[l]), bsz, tl["lru_tm"])
        y_dn = _gated_deltanet(proj, ba_col, ba_row, conv_qkv_w[l], dn_a_log[l], dn_dt_bias[l],
                               row(dn_out_norm[l]), bsz, tl["gdn_tm"], tl["gdn_hb"])
        xt, h2 = _out_proj(y_lru, y_dn, w_out[l].astype(BF16), xt, row(norm_ffn[l]),
                           tl["out_tm"], rows_as_tiles=is_moe)
        j = l // 2
        if not is_moe:
            nxt = moe_w_gate[j], moe_w_up[j], moe_w_down[j]
            steps = (t // tl["ffn_tm"]) * (ffn_w_gate.shape[2] // tl["ffn_tf"])
            flat = tuple(w.reshape(-1, w.shape[-1]) for w in nxt)
            ride = l + 1 < depth and all(
                a.shape[0] % CAST_ROWS == 0 and a.shape[0] // CAST_ROWS <= steps for a in flat)
            xt, cast = _ffn_dense(h2, xt, ffn_w_gate[j].astype(BF16), ffn_w_up[j].astype(BF16),
                                  ffn_w_down[j].astype(BF16), tl["ffn_tm"], tl["ffn_tf"],
                                  flat if ride else ())
            if ride:
                moe_bf16[j] = tuple(c.reshape(w.shape) for c, w in zip(cast, nxt))
        else:
            assert l == depth - 1, "the routed layer fuses the final RMSNorm"
            wg, wu, wd = moe_bf16.get(j) or (
                moe_w_gate[j].astype(BF16), moe_w_up[j].astype(BF16), moe_w_down[j].astype(BF16))
            out = _moe(xt, h2, row(norm_ffn[l]), moe_router[j], wg, wu, wd, row(norm_final), tl)
    return out.reshape(bsz, seq, d)
```

```python
import functools

import jax
import jax.numpy as jnp
from jax import lax
from jax.experimental import pallas as pl
from jax.experimental.pallas import tpu as pltpu

F32 = jnp.float32
BF16 = jnp.bfloat16
I32 = jnp.int32

D_MODEL = 2048
D_LRU = 1024
LRU_BLOCKS = 8
LRU_BLOCK_W = 128
LRU_C = 8.0
CONV_W = 4
D_DN = 1024
DN_HEAD_DIM = 128
DN_HEADS = 8
DN_CHUNK = 64
D_PROJ = 2 * D_LRU + 4 * D_DN
N_EXPERTS = 8
EPS = 1e-6

LANES = 128
SUBLANES = 8
ROW_CHUNKS = D_MODEL // LANES
VMEM_LIMIT = 56 * 1024 * 1024


def _params(*sem):
    return pltpu.CompilerParams(dimension_semantics=sem, vmem_limit_bytes=VMEM_LIMIT)


def _rms(x, gain):
    ms = jnp.mean(x * x, axis=-1, keepdims=True)
    return x * lax.rsqrt(ms + EPS) * gain


def _softplus(x):
    return jnp.maximum(x, 0.0) + jnp.log1p(jnp.exp(-jnp.abs(x)))


def _dot(a, b):
    return jnp.dot(a, b, preferred_element_type=F32)


def _dot_nt(a, b, precision=None):
    return lax.dot_general(a, b, (((1,), (1,)), ((), ())),
                           preferred_element_type=F32, precision=precision)


def _dot_tn(a, b):
    return lax.dot_general(a, b, (((0,), (0,)), ((), ())), preferred_element_type=F32)


def _inproj_kernel(x_ref, g_ref, w_ref, ws_ref, wst_ref, o_ref, oc_ref, or_ref, h_ref):
    @pl.when(pl.program_id(1) == 0)
    def _():
        h = _rms(x_ref[...], g_ref[...]).astype(BF16)
        h_ref[...] = h
        oc_ref[...] = _dot(h, ws_ref[...])
        or_ref[...] = _dot_nt(wst_ref[...], h)

    o_ref[...] = _dot(h_ref[...], w_ref[...])


def _in_proj(x, gain, w_main, w_small, w_small_t, tm, tn):
    t, d = x.shape
    n = w_main.shape[1]
    return pl.pallas_call(
        _inproj_kernel,
        grid=(t // tm, n // tn),
        in_specs=[
            pl.BlockSpec((tm, d), lambda i, j: (i, 0)),
            pl.BlockSpec((1, d), lambda i, j: (0, 0)),
            pl.BlockSpec((d, tn), lambda i, j: (0, j)),
            pl.BlockSpec((d, LANES), lambda i, j: (0, 0)),
            pl.BlockSpec((2 * DN_HEADS, d), lambda i, j: (0, 0)),
        ],
        out_specs=[
            pl.BlockSpec((tm, tn), lambda i, j: (i, j)),
            pl.BlockSpec((tm, LANES), lambda i, j: (i, 0)),
            pl.BlockSpec((2 * DN_HEADS, tm), lambda i, j: (0, i)),
        ],
        out_shape=[
            jax.ShapeDtypeStruct((t, n), F32),
            jax.ShapeDtypeStruct((t, LANES), F32),
            jax.ShapeDtypeStruct((2 * DN_HEADS, t), F32),
        ],
        scratch_shapes=[pltpu.VMEM((tm, d), BF16)],
        compiler_params=_params("parallel", "arbitrary"),
        name="in_proj",
    )(x, gain, w_main, w_small, w_small_t)


def _lru_kernel(x_ref, gate_ref, cw_ref, cb_ref, wr_ref, br_ref, wi_ref, bi_ref, lam_ref,
                on_ref, o_ref, xs_ref, hc_ref, y_ref, *, tm):
    @pl.when(pl.program_id(1) == 0)
    def _():
        xs_ref[0:SUBLANES, :] = jnp.zeros((SUBLANES, D_LRU), F32)
        hc_ref[...] = jnp.zeros_like(hc_ref)

    xs_ref[SUBLANES:SUBLANES + tm, :] = x_ref[...]
    row = lax.broadcasted_iota(I32, (tm, LRU_BLOCK_W), 0)
    ssq = jnp.zeros((tm, 1), F32)
    for n in range(LRU_BLOCKS):
        sl = slice(n * LRU_BLOCK_W, (n + 1) * LRU_BLOCK_W)
        x_ext = xs_ref[:, sl]
        xc = pltpu.roll(x_ext, CONV_W - 1, 0)[SUBLANES:, :] * cw_ref[0:1, sl]
        for k in range(1, CONV_W - 1):
            xc = xc + pltpu.roll(x_ext, CONV_W - 1 - k, 0)[SUBLANES:, :] * cw_ref[k:k + 1, sl]
        xc = xc + x_ext[SUBLANES:, :] * cw_ref[CONV_W - 1:CONV_W, sl]
        xc = xc + cb_ref[:, sl]
        xb = xc.astype(BF16)
        r = jax.nn.sigmoid(_dot(xb, wr_ref[n]) + br_ref[:, sl])
        i = jax.nn.sigmoid(_dot(xb, wi_ref[n]) + bi_ref[:, sl])
        log_a = -LRU_C * r * _softplus(-lam_ref[:, sl])
        a = jnp.exp(log_a)
        b = jnp.sqrt(-jnp.tanh(log_a) * (1.0 + a * a)) * (i * xc)
        d = 1
        while d < tm:
            keep = row >= d
            b = jnp.where(keep, a * pltpu.roll(b, d, 0) + b, b)
            a = jnp.where(keep, a * pltpu.roll(a, d, 0), a)
            d *= 2
        h = b + a * hc_ref[0:1, sl]
        hc_ref[0:1, sl] = h[tm - 1:tm, :]
        y = h * jax.nn.gelu(gate_ref[:, sl], approximate=True)
        y_ref[:, sl] = y
        ssq = ssq + jnp.sum(y * y, axis=-1, keepdims=True)
    inv = lax.rsqrt(ssq / D_LRU + EPS)
    o_ref[...] = (y_ref[...] * inv * on_ref[...]).astype(BF16)
    xs_ref[0:SUBLANES, :] = x_ref[tm - SUBLANES:tm, :]


def _rg_lru(proj, conv_w, conv_b, w_r, b_r, w_i, b_i, lam, out_norm, bsz, tm):
    t = proj.shape[0]
    ns = t // bsz // tm
    row = lambda b, s: (0, 0)
    return pl.pallas_call(
        functools.partial(_lru_kernel, tm=tm),
        grid=(bsz, ns),
        in_specs=[
            pl.BlockSpec((tm, D_LRU), lambda b, s: (b * ns + s, 0)),
            pl.BlockSpec((tm, D_LRU), lambda b, s: (b * ns + s, 1)),
            pl.BlockSpec((CONV_W, D_LRU), row),
            pl.BlockSpec((1, D_LRU), row),
            pl.BlockSpec((LRU_BLOCKS, LRU_BLOCK_W, LRU_BLOCK_W), lambda b, s: (0, 0, 0)),
            pl.BlockSpec((1, D_LRU), row),
            pl.BlockSpec((LRU_BLOCKS, LRU_BLOCK_W, LRU_BLOCK_W), lambda b, s: (0, 0, 0)),
            pl.BlockSpec((1, D_LRU), row),
            pl.BlockSpec((1, D_LRU), row),
            pl.BlockSpec((1, D_LRU), row),
        ],
        out_specs=pl.BlockSpec((tm, D_LRU), lambda b, s: (b * ns + s, 0)),
        out_shape=jax.ShapeDtypeStruct((t, D_LRU), BF16),
        scratch_shapes=[
            pltpu.VMEM((tm + SUBLANES, D_LRU), F32),
            pltpu.VMEM((SUBLANES, D_LRU), F32),
            pltpu.VMEM((tm, D_LRU), F32),
        ],
        compiler_params=_params("parallel", "arbitrary"),
        name="rg_lru",
    )(proj, proj, conv_w, conv_b, w_r, b_r, w_i, b_i, lam, out_norm)


def _gdn_kernel(q_ref, k_ref, v_ref, gate_ref, cwq_ref, cwk_ref, cwv_ref, bac_ref, bar_ref,
                alc_ref, dtc_ref, alr_ref, dtr_ref, on_ref, o_ref,
                qs_ref, ks_ref, vs_ref, st_ref, gr_ref, *, tm, hb):
    hg = pl.program_id(1)
    c = DN_CHUNK

    @pl.when(pl.program_id(2) == 0)
    def _():
        zeros = jnp.zeros((SUBLANES, hb * DN_HEAD_DIM), F32)
        qs_ref[0:SUBLANES, :] = zeros
        ks_ref[0:SUBLANES, :] = zeros
        vs_ref[0:SUBLANES, :] = zeros
        st_ref[...] = jnp.zeros_like(st_ref)

    def conv_silu(x_ref, xs_ref, cw_ref):
        xs_ref[SUBLANES:SUBLANES + tm, :] = x_ref[...]
        x_ext = xs_ref[...]
        y = pltpu.roll(x_ext, CONV_W - 1, 0)[SUBLANES:, :] * cw_ref[0:1, :]
        for kk in range(1, CONV_W - 1):
            y = y + pltpu.roll(x_ext, CONV_W - 1 - kk, 0)[SUBLANES:, :] * cw_ref[kk:kk + 1, :]
        y = y + x_ext[SUBLANES:, :] * cw_ref[CONV_W - 1:CONV_W, :]
        xs_ref[0:SUBLANES, :] = x_ref[tm - SUBLANES:tm, :]
        return jax.nn.silu(y)

    q_all = conv_silu(q_ref, qs_ref, cwq_ref)
    k_all = conv_silu(k_ref, ks_ref, cwk_ref)
    v_all = conv_silu(v_ref, vs_ref, cwv_ref)

    bac = bac_ref[...]
    lane = lax.broadcasted_iota(I32, (tm, LANES), 1)
    rowc = lax.broadcasted_iota(I32, (tm, LANES), 0) % c
    g_all = -jnp.exp(alc_ref[...]) * _softplus(bac + dtc_ref[...])
    d = 1
    while d < c:
        g_all = g_all + jnp.where(rowc >= d, pltpu.roll(g_all, d, 0), 0.0)
        d *= 2
    sig_all = jax.nn.sigmoid(bac)

    g_row = -jnp.exp(alr_ref[...]) * _softplus(bar_ref[DN_HEADS:2 * DN_HEADS, :] + dtr_ref[...])
    lanec = lax.broadcasted_iota(I32, (DN_HEADS, tm), 1) % c
    d = 1
    while d < c:
        g_row = g_row + jnp.where(lanec >= d, pltpu.roll(g_row, d, 1), 0.0)
        d *= 2
    gr_ref[...] = g_row

    ri = lax.broadcasted_iota(I32, (c, c), 0)
    ci = lax.broadcasted_iota(I32, (c, c), 1)
    causal = ri >= ci
    strict = ri > ci
    eye = (ri == ci).astype(F32)
    on = on_ref[...]
    n_ch = tm // c

    probs = []
    for j in range(hb):
        hd = hg * hb + j
        cs = slice(j * DN_HEAD_DIM, (j + 1) * DN_HEAD_DIM)
        q = q_all[:, cs]
        k = k_all[:, cs]
        q = q * lax.rsqrt(jnp.sum(q * q, axis=-1, keepdims=True) + EPS) * (DN_HEAD_DIM ** -0.5)
        k = k * lax.rsqrt(jnp.sum(k * k, axis=-1, keepdims=True) + EPS)
        gcum = jnp.sum(jnp.where(lane == hd + DN_HEADS, g_all, 0.0), axis=1, keepdims=True)
        beta = jnp.sum(jnp.where(lane == hd, sig_all, 0.0), axis=1, keepdims=True)
        grow_all = gr_ref[pl.ds(hd, 1), :]
        for ch in range(n_ch):
            rs = slice(ch * c, (ch + 1) * c)
            gcol = gcum[rs]
            kc = k[rs]
            kb = kc * beta[rs]
            probs.append(dict(
                j=j, ch=ch, qc=q[rs], kc=kc, kb=kb, vb=v_all[rs, cs] * beta[rs], gcol=gcol,
                eg=jnp.exp(gcol), glast=gcol[c - 1:c, :],
                decay=jnp.where(causal, jnp.exp(jnp.where(causal, gcol - grow_all[:, rs], 0.0)), 0.0)))
    for p in probs:
        p["prod"] = _dot_nt(jnp.concatenate([p["qc"], p["kb"]], axis=0).astype(BF16),
                            p["kc"].astype(BF16))
    for p in probs:
        p["qk"] = jnp.where(causal, p["prod"][0:c] * p["decay"], 0.0).astype(BF16)
        low = jnp.where(strict, p["prod"][c:2 * c] * p["decay"], 0.0)
        p["inv"] = eye - low
        p["pw"] = low.astype(BF16)
    n_sq = 1
    while 2 * n_sq < c:
        for p in probs:
            p["pw"] = _dot(p["pw"], p["pw"]).astype(BF16)
        for p in probs:
            p["inv"] = p["inv"] + _dot(p["inv"].astype(BF16), p["pw"])
        n_sq *= 2
    for p in probs:
        rhs = jnp.concatenate([p["vb"], p["kb"] * p["eg"]], axis=1).astype(BF16)
        sol = _dot(p["inv"].astype(BF16), rhs)
        p["u"] = sol[:, 0:DN_HEAD_DIM]
        w = sol[:, DN_HEAD_DIM:2 * DN_HEAD_DIM]
        p["wq"] = jnp.concatenate([w, p["qc"] * p["eg"]], axis=0).astype(BF16)
        p["k_dec"] = (p["kc"] * jnp.exp(p["glast"] - p["gcol"])).astype(BF16)
        p["g_end"] = jnp.exp(p["glast"])

    states = [st_ref[j] for j in range(hb)]
    for ch in range(n_ch):
        rs = slice(ch * c, (ch + 1) * c)
        cur = [probs[j * n_ch + ch] for j in range(hb)]
        wss = [_dot(p["wq"], states[j].astype(BF16)) for j, p in enumerate(cur)]
        v_nbs = [(p["u"] - ws[0:c]).astype(BF16) for p, ws in zip(cur, wss)]
        outs = [ws[c:2 * c] + _dot(p["qk"], v_nb) for p, ws, v_nb in zip(cur, wss, v_nbs)]
        states = [states[j] * p["g_end"] + _dot_tn(p["k_dec"], v_nb)
                  for j, (p, v_nb) in enumerate(zip(cur, v_nbs))]
        for j, o in enumerate(outs):
            cs = slice(j * DN_HEAD_DIM, (j + 1) * DN_HEAD_DIM)
            o_ref[rs, cs] = (_rms(o, on) * jax.nn.silu(gate_ref[rs, cs])).astype(BF16)
    for j in range(hb):
        st_ref[j] = states[j]


def _gated_deltanet(proj, ba_col, ba_row, conv_w, a_log, dt_bias, out_norm, bsz, tm, hb):
    t = proj.shape[0]
    ns = t // bsz // tm
    hh = DN_HEADS
    ng = hh // hb
    wb = hb * DN_HEAD_DIM
    q0 = 2 * D_LRU // wb
    pad = jnp.zeros((hh,), F32)
    al_c = jnp.concatenate([pad, a_log, jnp.zeros((LANES - 2 * hh,), F32)]).reshape(1, LANES)
    dt_c = jnp.concatenate([pad, dt_bias, jnp.zeros((LANES - 2 * hh,), F32)]).reshape(1, LANES)
    al_r = a_log.reshape(hh, 1)
    dt_r = dt_bias.reshape(hh, 1)
    tok = lambda col0: (lambda b, h, s: (b * ns + s, col0 + h))
    cw = lambda col0: (lambda b, h, s: (0, col0 + h))
    const = lambda b, h, s: (0, 0)
    return pl.pallas_call(
        functools.partial(_gdn_kernel, tm=tm, hb=hb),
        grid=(bsz, ng, ns),
        in_specs=[
            pl.BlockSpec((tm, wb), tok(q0)),
            pl.BlockSpec((tm, wb), tok(q0 + ng)),
            pl.BlockSpec((tm, wb), tok(q0 + 2 * ng)),
            pl.BlockSpec((tm, wb), tok(q0 + 3 * ng)),
            pl.BlockSpec((CONV_W, wb), cw(0)),
            pl.BlockSpec((CONV_W, wb), cw(ng)),
            pl.BlockSpec((CONV_W, wb), cw(2 * ng)),
            pl.BlockSpec((tm, LANES), lambda b, h, s: (b * ns + s, 0)),
            pl.BlockSpec((2 * hh, tm), lambda b, h, s: (0, b * ns + s)),
            pl.BlockSpec((1, LANES), const),
            pl.BlockSpec((1, LANES), const),
            pl.BlockSpec((hh, 1), const),
            pl.BlockSpec((hh, 1), const),
            pl.BlockSpec((1, DN_HEAD_DIM), const),
        ],
        out_specs=pl.BlockSpec((tm, wb), lambda b, h, s: (b * ns + s, h)),
        out_shape=jax.ShapeDtypeStruct((t, D_DN), BF16),
        scratch_shapes=[
            pltpu.VMEM((tm + SUBLANES, wb), F32),
            pltpu.VMEM((tm + SUBLANES, wb), F32),
            pltpu.VMEM((tm + SUBLANES, wb), F32),
            pltpu.VMEM((hb, DN_HEAD_DIM, DN_HEAD_DIM), F32),
            pltpu.VMEM((hh, tm), F32),
        ],
        compiler_params=_params("parallel", "parallel", "arbitrary"),
        name="gated_deltanet",
    )(proj, proj, proj, proj, conv_w, conv_w, conv_w, ba_col, ba_row,
      al_c, dt_c, al_r, dt_r, out_norm)


def _outproj_kernel(yl_ref, yd_ref, w_ref, x_ref, g_ref, xo_ref, h_ref, *, rows_as_tiles):
    acc = _dot(yl_ref[...], w_ref[0:D_LRU, :]) + _dot(yd_ref[...], w_ref[D_LRU:D_LRU + D_DN, :])
    xn = x_ref[...] + acc
    xo_ref[...] = xn
    hn = _rms(xn, g_ref[...])
    if rows_as_tiles:
        for cc in range(ROW_CHUNKS):
            h_ref[:, cc, :] = hn[:, cc * LANES:(cc + 1) * LANES]
    else:
        h_ref[...] = hn.astype(h_ref.dtype)


def _out_proj(y_lru, y_dn, w, x, gain, tm, rows_as_tiles):
    t, d = x.shape
    if rows_as_tiles:
        h_spec = pl.BlockSpec((tm, ROW_CHUNKS, LANES), lambda i: (i, 0, 0))
        h_shape = jax.ShapeDtypeStruct((t, ROW_CHUNKS, LANES), F32)
    else:
        h_spec = pl.BlockSpec((tm, d), lambda i: (i, 0))
        h_shape = jax.ShapeDtypeStruct((t, d), BF16)
    return pl.pallas_call(
        functools.partial(_outproj_kernel, rows_as_tiles=rows_as_tiles),
        grid=(t // tm,),
        in_specs=[
            pl.BlockSpec((tm, D_LRU), lambda i: (i, 0)),
            pl.BlockSpec((tm, D_DN), lambda i: (i, 0)),
            pl.BlockSpec((D_LRU + D_DN, d), lambda i: (0, 0)),
            pl.BlockSpec((tm, d), lambda i: (i, 0)),
            pl.BlockSpec((1, d), lambda i: (0, 0)),
        ],
        out_specs=[pl.BlockSpec((tm, d), lambda i: (i, 0)), h_spec],
        out_shape=[jax.ShapeDtypeStruct((t, d), F32), h_shape],
        compiler_params=_params("parallel"),
        name="out_proj",
    )(y_lru, y_dn, w, x, gain)


CAST_ROWS = 64


def _ffn_kernel(h_ref, x_ref, wg_ref, wu_ref, wd_ref, *rest, cast_blocks):
    n_cast = len(cast_blocks)
    src_refs, o_ref, dst_refs = rest[:n_cast], rest[n_cast], rest[n_cast + 1:]

    @pl.when(pl.program_id(1) == 0)
    def _():
        o_ref[...] = x_ref[...]

    h = h_ref[...]
    g = _dot(h, wg_ref[...])
    u = _dot(h, wu_ref[...])
    o_ref[...] += _dot((jax.nn.silu(g) * u).astype(BF16), wd_ref[...])

    step = pl.program_id(0) * pl.num_programs(1) + pl.program_id(1)
    for src, dst, nb in zip(src_refs, dst_refs, cast_blocks):
        @pl.when(step < nb)
        def _(src=src, dst=dst):
            dst[...] = src[...].astype(BF16)


def _ffn_dense(h, x, wg, wu, wd, tm, tf, to_cast=()):
    t, d = x.shape
    ff = wg.shape[1]
    nf = ff // tf
    cast_blocks = tuple(a.shape[0] // CAST_ROWS for a in to_cast)
    assert all(nb <= (t // tm) * nf for nb in cast_blocks)
    cast_spec = lambda a, nb: pl.BlockSpec(
        (CAST_ROWS, a.shape[1]), lambda i, f: (jnp.minimum(i * nf + f, nb - 1), 0))
    cast_specs = [cast_spec(a, nb) for a, nb in zip(to_cast, cast_blocks)]
    res = pl.pallas_call(
        functools.partial(_ffn_kernel, cast_blocks=cast_blocks),
        grid=(t // tm, nf),
        in_specs=[
            pl.BlockSpec((tm, d), lambda i, f: (i, 0)),
            pl.BlockSpec((tm, d), lambda i, f: (i, 0)),
            pl.BlockSpec((d, tf), lambda i, f: (0, f)),
            pl.BlockSpec((d, tf), lambda i, f: (0, f)),
            pl.BlockSpec((tf, d), lambda i, f: (f, 0)),
        ] + cast_specs,
        out_specs=[pl.BlockSpec((tm, d), lambda i, f: (i, 0))] + cast_specs,
        out_shape=[jax.ShapeDtypeStruct((t, d), F32)]
        + [jax.ShapeDtypeStruct(a.shape, BF16) for a in to_cast],
        compiler_params=_params("arbitrary", "arbitrary"),
        name="ffn_dense",
    )(h, x, wg, wu, wd, *to_cast)
    return res[0], tuple(res[1:])


R_E1, R_E2, R_W1, R_W2, R_RANK1, R_RANK2 = range(6)


def _router_kernel(x_ref, g_ref, rt_ref, mr_ref, mc_ref, cnt_ref, carry_ref, *, tm):
    @pl.when(pl.program_id(0) == 0)
    def _():
        carry_ref[...] = jnp.zeros_like(carry_ref)

    h = _rms(x_ref[...], g_ref[...])
    logits = _dot_nt(rt_ref[...], h, precision=lax.Precision.HIGHEST)
    eidx = lax.broadcasted_iota(I32, (N_EXPERTS, tm), 0)
    m1 = jnp.max(logits, axis=0, keepdims=True)
    i1 = jnp.min(jnp.where(logits == m1, eidx, N_EXPERTS), axis=0, keepdims=True)
    rest = jnp.where(eidx == i1, -jnp.inf, logits)
    m2 = jnp.max(rest, axis=0, keepdims=True)
    i2 = jnp.min(jnp.where(rest == m2, eidx, N_EXPERTS), axis=0, keepdims=True)
    e2 = jnp.exp(m2 - m1)
    w1 = 1.0 / (1.0 + e2)
    w2 = e2 / (1.0 + e2)
    oh1 = eidx == i1
    oh2 = eidx == i2
    cnt = jnp.where(oh1 | oh2, 1.0, 0.0)
    ti = lax.broadcasted_iota(I32, (tm, tm), 0)
    tj = lax.broadcasted_iota(I32, (tm, tm), 1)
    upper = jnp.where(ti <= tj, 1.0, 0.0).astype(BF16)
    cum = _dot(cnt.astype(BF16), upper)
    before = cum - cnt + carry_ref[:, 0:1]
    rank1 = jnp.sum(jnp.where(oh1, before, 0.0), axis=0, keepdims=True)
    rank2 = jnp.sum(jnp.where(oh2, before, 0.0), axis=0, keepdims=True)
    total = carry_ref[:, 0:1] + cum[:, tm - 1:tm]
    carry_ref[...] = jnp.broadcast_to(total, carry_ref.shape)
    cnt_ref[...] = jnp.broadcast_to(total, cnt_ref.shape).astype(I32)
    rec = jnp.concatenate(
        [i1.astype(F32), i2.astype(F32), w1, w2, rank1, rank2,
         jnp.zeros((LANES - 6, tm), F32)], axis=0)
    mr_ref[...] = rec[0:SUBLANES, :]
    mc_ref[...] = rec.T


def _router(x, gain, router_t, tm):
    t, d = x.shape
    return pl.pallas_call(
        functools.partial(_router_kernel, tm=tm),
        grid=(t // tm,),
        in_specs=[
            pl.BlockSpec((tm, d), lambda i: (i, 0)),
            pl.BlockSpec((1, d), lambda i: (0, 0)),
            pl.BlockSpec((N_EXPERTS, d), lambda i: (0, 0)),
        ],
        out_specs=[
            pl.BlockSpec((SUBLANES, tm), lambda i: (0, i)),
            pl.BlockSpec((tm, LANES), lambda i: (i, 0)),
            pl.BlockSpec((N_EXPERTS, LANES), lambda i: (0, 0)),
        ],
        out_shape=[
            jax.ShapeDtypeStruct((SUBLANES, t), F32),
            jax.ShapeDtypeStruct((t, LANES), F32),
            jax.ShapeDtypeStruct((N_EXPERTS, LANES), I32),
        ],
        scratch_shapes=[pltpu.VMEM((N_EXPERTS, LANES), F32)],
        compiler_params=_params("arbitrary"),
        name="moe_router",
    )(x, gain, router_t)


DMA_UNROLL = 8


def _dispatch_kernel(d1_ref, d2_ref, zb_ref, h_ref, xs_ref, z_ref, sem, zsem, *, tm, tmb, n_zero):
    base = pl.program_id(0) * tm

    @pl.when(pl.program_id(0) == 0)
    def _():
        z_ref[...] = jnp.zeros_like(z_ref)
        for e in range(n_zero):
            fill = pltpu.make_async_copy(z_ref, xs_ref.at[pl.ds(zb_ref[e], tmb)], zsem)
            fill.start()
            fill.wait()

    def issue(g, carry):
        for u in range(DMA_UNROLL):
            r = g * DMA_UNROLL + u
            pltpu.make_async_copy(h_ref.at[r], xs_ref.at[d1_ref[base + r]], sem.at[0]).start()
            pltpu.make_async_copy(h_ref.at[r], xs_ref.at[d2_ref[base + r]], sem.at[1]).start()
        return carry

    lax.fori_loop(0, tm // DMA_UNROLL, issue, 0)
    pltpu.make_async_copy(h_ref, xs_ref.at[pl.ds(0, tm)], sem.at[0]).wait()
    pltpu.make_async_copy(h_ref, xs_ref.at[pl.ds(0, tm)], sem.at[1]).wait()


def _dispatch(dest1, dest2, zero_blocks, h_tiles, n_slots, tm, tmb):
    t = h_tiles.shape[0]
    return pl.pallas_call(
        functools.partial(_dispatch_kernel, tm=tm, tmb=tmb, n_zero=zero_blocks.shape[0]),
        grid_spec=pltpu.PrefetchScalarGridSpec(
            num_scalar_prefetch=3,
            grid=(t // tm,),
            in_specs=[pl.BlockSpec((tm, ROW_CHUNKS, LANES), lambda i, d1, d2, zb: (i, 0, 0))],
            out_specs=pl.BlockSpec(memory_space=pl.ANY),
            scratch_shapes=[
                pltpu.VMEM((tmb, ROW_CHUNKS, LANES), F32),
                pltpu.SemaphoreType.DMA((2,)),
                pltpu.SemaphoreType.DMA(()),
            ],
        ),
        out_shape=jax.ShapeDtypeStruct((n_slots, ROW_CHUNKS, LANES), F32),
        compiler_params=_params("arbitrary"),
        name="moe_dispatch",
    )(dest1, dest2, zero_blocks, h_tiles)


def _gmm_kernel(be_ref, nu_ref, xs_ref, wg_ref, wu_ref, wd_ref, y_ref, xb_ref, acc_ref):
    del be_ref
    f = pl.program_id(1)
    used = pl.program_id(0) < nu_ref[0]

    @pl.when(jnp.logical_and(jnp.logical_not(used), f == pl.num_programs(1) - 1))
    def _():
        y_ref[...] = jnp.zeros_like(y_ref)

    @pl.when(used)
    def _():
        @pl.when(f == 0)
        def _():
            xb_ref[...] = jnp.concatenate(
                [xs_ref[:, cc, :] for cc in range(ROW_CHUNKS)], axis=1).astype(BF16)
            acc_ref[...] = jnp.zeros_like(acc_ref)

        xb = xb_ref[...]
        g = _dot(xb, wg_ref[...])
        u = _dot(xb, wu_ref[...])
        acc_ref[...] += _dot((jax.nn.silu(g) * u).astype(BF16), wd_ref[...])

        @pl.when(f == pl.num_programs(1) - 1)
        def _():
            for cc in range(ROW_CHUNKS):
                y_ref[:, cc, :] = acc_ref[:, cc * LANES:(cc + 1) * LANES]


def _gmm(block_expert, n_used, xs, wg, wu, wd, tm, tf):
    n_slots = xs.shape[0]
    d = D_MODEL
    ff = wg.shape[2]
    nf = ff // tf

    def blk(b, nu):
        return jnp.minimum(b, nu[0] - 1)

    def fidx(b, f, nu):
        return jnp.where(b < nu[0], f, nf - 1)

    return pl.pallas_call(
        _gmm_kernel,
        grid_spec=pltpu.PrefetchScalarGridSpec(
            num_scalar_prefetch=2,
            grid=(n_slots // tm, nf),
            in_specs=[
                pl.BlockSpec((tm, ROW_CHUNKS, LANES), lambda b, f, be, nu: (blk(b, nu), 0, 0)),
                pl.BlockSpec((None, d, tf), lambda b, f, be, nu: (be[blk(b, nu)], 0, fidx(b, f, nu))),
                pl.BlockSpec((None, d, tf), lambda b, f, be, nu: (be[blk(b, nu)], 0, fidx(b, f, nu))),
                pl.BlockSpec((None, tf, d), lambda b, f, be, nu: (be[blk(b, nu)], fidx(b, f, nu), 0)),
            ],
            out_specs=pl.BlockSpec((tm, ROW_CHUNKS, LANES), lambda b, f, be, nu: (b, 0, 0)),
            scratch_shapes=[pltpu.VMEM((tm, d), BF16), pltpu.VMEM((tm, d), F32)],
        ),
        out_shape=jax.ShapeDtypeStruct((n_slots, ROW_CHUNKS, LANES), F32),
        compiler_params=_params("arbitrary", "arbitrary"),
        name="moe_gmm",
    )(block_expert, n_used, xs, wg, wu, wd)


def _combine_kernel(d1_ref, d2_ref, x_ref, mc_ref, y_ref, g_ref, o_ref, b1_ref, b2_ref, sem, *, tm):
    i = pl.program_id(0)
    slot = i % 2

    def gather(step, sl):
        base = step * tm

        def issue(g, carry):
            for u in range(DMA_UNROLL):
                r = g * DMA_UNROLL + u
                pltpu.make_async_copy(y_ref.at[d1_ref[base + r]], b1_ref.at[sl, r], sem.at[sl, 0]).start()
                pltpu.make_async_copy(y_ref.at[d2_ref[base + r]], b2_ref.at[sl, r], sem.at[sl, 1]).start()
            return carry

        lax.fori_loop(0, tm // DMA_UNROLL, issue, 0)

    @pl.when(i == 0)
    def _():
        gather(0, 0)

    @pl.when(i + 1 < pl.num_programs(0))
    def _():
        gather(i + 1, 1 - slot)

    pltpu.make_async_copy(y_ref.at[pl.ds(0, tm)], b1_ref.at[slot], sem.at[slot, 0]).wait()
    pltpu.make_async_copy(y_ref.at[pl.ds(0, tm)], b2_ref.at[slot], sem.at[slot, 1]).wait()
    w1 = mc_ref[:, R_W1:R_W1 + 1]
    w2 = mc_ref[:, R_W2:R_W2 + 1]
    xn = jnp.concatenate(
        [x_ref[:, cc * LANES:(cc + 1) * LANES]
         + (w1 * b1_ref[slot, :, cc, :] + w2 * b2_ref[slot, :, cc, :])
         for cc in range(ROW_CHUNKS)], axis=1)
    o_ref[...] = _rms(xn, g_ref[...])


def _combine(dest1, dest2, x, meta_col, y, gain, tm):
    t, d = x.shape
    return pl.pallas_call(
        functools.partial(_combine_kernel, tm=tm),
        grid_spec=pltpu.PrefetchScalarGridSpec(
            num_scalar_prefetch=2,
            grid=(t // tm,),
            in_specs=[
                pl.BlockSpec((tm, d), lambda i, d1, d2: (i, 0)),
                pl.BlockSpec((tm, LANES), lambda i, d1, d2: (i, 0)),
                pl.BlockSpec(memory_space=pl.ANY),
                pl.BlockSpec((1, d), lambda i, d1, d2: (0, 0)),
            ],
            out_specs=pl.BlockSpec((tm, d), lambda i, d1, d2: (i, 0)),
            scratch_shapes=[
                pltpu.VMEM((2, tm, ROW_CHUNKS, LANES), F32),
                pltpu.VMEM((2, tm, ROW_CHUNKS, LANES), F32),
                pltpu.SemaphoreType.DMA((2, 2)),
            ],
        ),
        out_shape=jax.ShapeDtypeStruct((t, d), F32),
        compiler_params=_params("arbitrary"),
        name="moe_combine",
    )(dest1, dest2, x, meta_col, y, gain)


def _tiles(t, seq):
    pick = lambda want, total: want if total % want == 0 else total
    return dict(
        proj_tm=pick(1024, t), proj_tn=1024,
        lru_tm=pick(256, seq),
        gdn_tm=pick(512, seq), gdn_hb=4,
        out_tm=pick(512, t),
        ffn_tm=pick(512, t), ffn_tf=512,
        router_tm=pick(512, t),
        dispatch_tm=pick(256, t),
        gmm_tm=512, gmm_tf=512,
        combine_tm=pick(256, t),
    )


def _moe(x, h_tiles, norm_ffn, router, wg, wu, wd, norm_final, tl):
    t = x.shape[0]
    tmb = tl["gmm_tm"]
    meta_row, meta_col, counts = _router(x, norm_ffn, router.T, tl["router_tm"])
    counts = counts[:, 0]
    padded = (counts + tmb - 1) // tmb * tmb
    pad_end = jnp.cumsum(padded)
    pad_start = pad_end - padded
    e1 = meta_row[R_E1].astype(I32)
    e2 = meta_row[R_E2].astype(I32)
    experts = jnp.arange(N_EXPERTS, dtype=I32)[:, None]
    start1 = jnp.sum(jnp.where(e1[None, :] == experts, pad_start[:, None], 0), axis=0)
    start2 = jnp.sum(jnp.where(e2[None, :] == experts, pad_start[:, None], 0), axis=0)
    dest1 = start1 + meta_row[R_RANK1].astype(I32)
    dest2 = start2 + meta_row[R_RANK2].astype(I32)
    n_slots = 2 * t + N_EXPERTS * tmb
    n_blocks = n_slots // tmb
    block_start = jnp.arange(n_blocks, dtype=I32) * tmb
    block_expert = jnp.minimum(
        jnp.sum((block_start[:, None] >= pad_end[None, :]).astype(I32), axis=1), N_EXPERTS - 1)
    n_used = (pad_end[-1] // tmb).astype(I32).reshape(1)
    zero_blocks = jnp.concatenate([
        jnp.maximum(pad_end - tmb, 0),
        jnp.minimum(pad_end[-1] + jnp.arange(N_EXPERTS, dtype=I32) * tmb, n_slots - tmb),
    ]).astype(I32)
    xs = _dispatch(dest1, dest2, zero_blocks, h_tiles, n_slots, tl["dispatch_tm"], tmb)
    y = _gmm(block_expert, n_used, xs, wg, wu, wd, tmb, tl["gmm_tf"])
    return _combine(dest1, dest2, x, meta_col, y, norm_final, tl["combine_tm"])


def kernel(x, norm_mix, w_in, conv_lru_w, conv_lru_b, lru_w_r, lru_b_r, lru_w_i, lru_b_i,
           lru_lambda, lru_out_norm, conv_qkv_w, dn_a_log, dn_dt_bias, dn_out_norm, w_out,
           norm_ffn, ffn_w_gate, ffn_w_up, ffn_w_down, moe_router, moe_w_gate, moe_w_up,
           moe_w_down, norm_final):
    bsz, seq, d = x.shape
    t = bsz * seq
    depth = w_in.shape[0]
    tl = _tiles(t, seq)
    row = lambda v: v.reshape(1, -1)
    xt = x.reshape(t, d)
    out = None
    moe_bf16 = {}
    for l in range(depth):
        is_moe = l % 2 == 1
        w_main = w_in[l, :, :D_PROJ].astype(BF16)
        w_small = w_in[l, :, D_PROJ:]
        w_small_p = jnp.pad(w_small, ((0, 0), (0, LANES - 2 * DN_HEADS))).astype(BF16)
        proj, ba_col, ba_row = _in_proj(xt, row(norm_mix[l]), w_main, w_small_p,
                                        w_small.T.astype(BF16), tl["proj_tm"], tl["proj_tn"])
        y_lru = _rg_lru(proj, conv_lru_w[l], row(conv_lru_b[l]), lru_w_r[l].astype(BF16),
                        row(lru_b_r[l]), lru_w_i[l].astype(BF16), row(lru_b_i[l]),
                        row(lru_lambda[l]), row(lru_out_norm[l]), bsz, tl["lru_tm"])
        y_dn = _gated_deltanet(proj, ba_col, ba_row, conv_qkv_w[l], dn_a_log[l], dn_dt_bias[l],
                               row(dn_out_norm[l]), bsz, tl["gdn_tm"], tl["gdn_hb"])
        xt, h2 = _out_proj(y_lru, y_dn, w_out[l].astype(BF16), xt, row(norm_ffn[l]),
                           tl["out_tm"], rows_as_tiles=is_moe)
        j = l // 2
        if not is_moe:
            nxt = moe_w_gate[j], moe_w_up[j], moe_w_down[j]
            steps = (t // tl["ffn_tm"]) * (ffn_w_gate.shape[2] // tl["ffn_tf"])
            flat = tuple(w.reshape(-1, w.shape[-1]) for w in nxt)
            ride = l + 1 < depth and all(
                a.shape[0] % CAST_ROWS == 0 and a.shape[0] // CAST_ROWS <= steps for a in flat)
            xt, cast = _ffn_dense(h2, xt, ffn_w_gate[j].astype(BF16), ffn_w_up[j].astype(BF16),
                                  ffn_w_down[j].astype(BF16), tl["ffn_tm"], tl["ffn_tf"],
                                  flat if ride else ())
            if ride:
                moe_bf16[j] = tuple(c.reshape(w.shape) for c, w in zip(cast, nxt))
        else:
            assert l == depth - 1, "the routed layer fuses the final RMSNorm"
            wg, wu, wd = moe_bf16.get(j) or (
                moe_w_gate[j].astype(BF16), moe_w_up[j].astype(BF16), moe_w_down[j].astype(BF16))
            out = _moe(xt, h2, row(norm_ffn[l]), moe_router[j], wg, wu, wd, row(norm_final), tl)
    return out.reshape(bsz, seq, d)
```

```python
import functools

import jax
import jax.numpy as jnp
from jax import lax
from jax.experimental import pallas as pl
from jax.experimental.pallas import tpu as pltpu

F32 = jnp.float32
BF16 = jnp.bfloat16
I32 = jnp.int32

D_MODEL = 2048
D_LRU = 1024
LRU_BLOCKS = 8
LRU_BLOCK_W = 128
LRU_C = 8.0
CONV_W = 4
D_DN = 1024
DN_HEAD_DIM = 128
DN_HEADS = 8
DN_CHUNK = 64
D_PROJ = 2 * D_LRU + 4 * D_DN
N_EXPERTS = 8
EPS = 1e-6

LANES = 128
SUBLANES = 8
ROW_CHUNKS = D_MODEL // LANES
VMEM_LIMIT = 56 * 1024 * 1024


def _params(*sem):
    return pltpu.CompilerParams(dimension_semantics=sem, vmem_limit_bytes=VMEM_LIMIT)


def _rms(x, gain):
    ms = jnp.mean(x * x, axis=-1, keepdims=True)
    return x * lax.rsqrt(ms + EPS) * gain


def _softplus(x):
    return jnp.maximum(x, 0.0) + jnp.log1p(jnp.exp(-jnp.abs(x)))


def _dot(a, b):
    return jnp.dot(a, b, preferred_element_type=F32)


def _dot_nt(a, b, precision=None):
    return lax.dot_general(a, b, (((1,), (1,)), ((), ())),
                           preferred_element_type=F32, precision=precision)


def _dot_tn(a, b):
    return lax.dot_general(a, b, (((0,), (0,)), ((), ())), preferred_element_type=F32)


def _inproj_kernel(x_ref, g_ref, w_ref, ws_ref, wst_ref, o_ref, oc_ref, or_ref, h_ref):
    @pl.when(pl.program_id(1) == 0)
    def _():
        h = _rms(x_ref[...], g_ref[...]).astype(BF16)
        h_ref[...] = h
        oc_ref[...] = _dot(h, ws_ref[...])
        or_ref[...] = _dot_nt(wst_ref[...], h)

    o_ref[...] = _dot(h_ref[...], w_ref[...])


def _in_proj(x, gain, w_main, w_small, w_small_t, tm, tn):
    t, d = x.shape
    n = w_main.shape[1]
    return pl.pallas_call(
        _inproj_kernel,
        grid=(t // tm, n // tn),
        in_specs=[
            pl.BlockSpec((tm, d), lambda i, j: (i, 0)),
            pl.BlockSpec((1, d), lambda i, j: (0, 0)),
            pl.BlockSpec((d, tn), lambda i, j: (0, j)),
            pl.BlockSpec((d, LANES), lambda i, j: (0, 0)),
            pl.BlockSpec((2 * DN_HEADS, d), lambda i, j: (0, 0)),
        ],
        out_specs=[
            pl.BlockSpec((tm, tn), lambda i, j: (i, j)),
            pl.BlockSpec((tm, LANES), lambda i, j: (i, 0)),
            pl.BlockSpec((2 * DN_HEADS, tm), lambda i, j: (0, i)),
        ],
        out_shape=[
            jax.ShapeDtypeStruct((t, n), F32),
            jax.ShapeDtypeStruct((t, LANES), F32),
            jax.ShapeDtypeStruct((2 * DN_HEADS, t), F32),
        ],
        scratch_shapes=[pltpu.VMEM((tm, d), BF16)],
        compiler_params=_params("parallel", "arbitrary"),
        name="in_proj",
    )(x, gain, w_main, w_small, w_small_t)


def _lru_kernel(x_ref, gate_ref, cw_ref, cb_ref, wr_ref, br_ref, wi_ref, bi_ref, lam_ref,
                on_ref, o_ref, xs_ref, hc_ref, y_ref, *, tm):
    @pl.when(pl.program_id(1) == 0)
    def _():
        xs_ref[0:SUBLANES, :] = jnp.zeros((SUBLANES, D_LRU), F32)
        hc_ref[...] = jnp.zeros_like(hc_ref)

    xs_ref[SUBLANES:SUBLANES + tm, :] = x_ref[...]
    row = lax.broadcasted_iota(I32, (tm, LRU_BLOCK_W), 0)
    ssq = jnp.zeros((tm, 1), F32)
    for n in range(LRU_BLOCKS):
        sl = slice(n * LRU_BLOCK_W, (n + 1) * LRU_BLOCK_W)
        x_ext = xs_ref[:, sl]
        xc = pltpu.roll(x_ext, CONV_W - 1, 0)[SUBLANES:, :] * cw_ref[0:1, sl]
        for k in range(1, CONV_W - 1):
            xc = xc + pltpu.roll(x_ext, CONV_W - 1 - k, 0)[SUBLANES:, :] * cw_ref[k:k + 1, sl]
        xc = xc + x_ext[SUBLANES:, :] * cw_ref[CONV_W - 1:CONV_W, sl]
        xc = xc + cb_ref[:, sl]
        xb = xc.astype(BF16)
        r = jax.nn.sigmoid(_dot(xb, wr_ref[n]) + br_ref[:, sl])
        i = jax.nn.sigmoid(_dot(xb, wi_ref[n]) + bi_ref[:, sl])
        log_a = -LRU_C * r * _softplus(-lam_ref[:, sl])
        a = jnp.exp(log_a)
        b = jnp.sqrt(-jnp.tanh(log_a) * (1.0 + a * a)) * (i * xc)
        d = 1
        while d < tm:
            keep = row >= d
            b = jnp.where(keep, a * pltpu.roll(b, d, 0) + b, b)
            a = jnp.where(keep, a * pltpu.roll(a, d, 0), a)
            d *= 2
        h = b + a * hc_ref[0:1, sl]
        hc_ref[0:1, sl] = h[tm - 1:tm, :]
        y = h * jax.nn.gelu(gate_ref[:, sl], approximate=True)
        y_ref[:, sl] = y
        ssq = ssq + jnp.sum(y * y, axis=-1, keepdims=True)
    inv = lax.rsqrt(ssq / D_LRU + EPS)
    o_ref[...] = (y_ref[...] * inv * on_ref[...]).astype(BF16)
    xs_ref[0:SUBLANES, :] = x_ref[tm - SUBLANES:tm, :]


def _rg_lru(proj, conv_w, conv_b, w_r, b_r, w_i, b_i, lam, out_norm, bsz, tm):
    t = proj.shape[0]
    ns = t // bsz // tm
    row = lambda b, s: (0, 0)
    return pl.pallas_call(
        functools.partial(_lru_kernel, tm=tm),
        grid=(bsz, ns),
        in_specs=[
            pl.BlockSpec((tm, D_LRU), lambda b, s: (b * ns + s, 0)),
            pl.BlockSpec((tm, D_LRU), lambda b, s: (b * ns + s, 1)),
            pl.BlockSpec((CONV_W, D_LRU), row),
            pl.BlockSpec((1, D_LRU), row),
            pl.BlockSpec((LRU_BLOCKS, LRU_BLOCK_W, LRU_BLOCK_W), lambda b, s: (0, 0, 0)),
            pl.BlockSpec((1, D_LRU), row),
            pl.BlockSpec((LRU_BLOCKS, LRU_BLOCK_W, LRU_BLOCK_W), lambda b, s: (0, 0, 0)),
            pl.BlockSpec((1, D_LRU), row),
            pl.BlockSpec((1, D_LRU), row),
            pl.BlockSpec((1, D_LRU), row),
        ],
        out_specs=pl.BlockSpec((tm, D_LRU), lambda b, s: (b * ns + s, 0)),
        out_shape=jax.ShapeDtypeStruct((t, D_LRU), BF16),
        scratch_shapes=[
            pltpu.VMEM((tm + SUBLANES, D_LRU), F32),
            pltpu.VMEM((SUBLANES, D_LRU), F32),
            pltpu.VMEM((tm, D_LRU), F32),
        ],
        compiler_params=_params("parallel", "arbitrary"),
        name="rg_lru",
    )(proj, proj, conv_w, conv_b, w_r, b_r, w_i, b_i, lam, out_norm)


def _gdn_kernel(q_ref, k_ref, v_ref, gate_ref, cwq_ref, cwk_ref, cwv_ref, bac_ref, bar_ref,
                alc_ref, dtc_ref, alr_ref, dtr_ref, on_ref, o_ref,
                qs_ref, ks_ref, vs_ref, st_ref, gr_ref, *, tm, hb):
    hg = pl.program_id(1)
    c = DN_CHUNK

    @pl.when(pl.program_id(2) == 0)
    def _():
        zeros = jnp.zeros((SUBLANES, hb * DN_HEAD_DIM), F32)
        qs_ref[0:SUBLANES, :] = zeros
        ks_ref[0:SUBLANES, :] = zeros
        vs_ref[0:SUBLANES, :] = zeros
        st_ref[...] = jnp.zeros_like(st_ref)

    def conv_silu(x_ref, xs_ref, cw_ref):
        xs_ref[SUBLANES:SUBLANES + tm, :] = x_ref[...]
        x_ext = xs_ref[...]
        y = pltpu.roll(x_ext, CONV_W - 1, 0)[SUBLANES:, :] * cw_ref[0:1, :]
        for kk in range(1, CONV_W - 1):
            y = y + pltpu.roll(x_ext, CONV_W - 1 - kk, 0)[SUBLANES:, :] * cw_ref[kk:kk + 1, :]
        y = y + x_ext[SUBLANES:, :] * cw_ref[CONV_W - 1:CONV_W, :]
        xs_ref[0:SUBLANES, :] = x_ref[tm - SUBLANES:tm, :]
        return jax.nn.silu(y)

    q_all = conv_silu(q_ref, qs_ref, cwq_ref)
    k_all = conv_silu(k_ref, ks_ref, cwk_ref)
    v_all = conv_silu(v_ref, vs_ref, cwv_ref)

    bac = bac_ref[...]
    lane = lax.broadcasted_iota(I32, (tm, LANES), 1)
    rowc = lax.broadcasted_iota(I32, (tm, LANES), 0) % c
    g_all = -jnp.exp(alc_ref[...]) * _softplus(bac + dtc_ref[...])
    d = 1
    while d < c:
        g_all = g_all + jnp.where(rowc >= d, pltpu.roll(g_all, d, 0), 0.0)
        d *= 2
    sig_all = jax.nn.sigmoid(bac)

    g_row = -jnp.exp(alr_ref[...]) * _softplus(bar_ref[DN_HEADS:2 * DN_HEADS, :] + dtr_ref[...])
    lanec = lax.broadcasted_iota(I32, (DN_HEADS, tm), 1) % c
    d = 1
    while d < c:
        g_row = g_row + jnp.where(lanec >= d, pltpu.roll(g_row, d, 1), 0.0)
        d *= 2
    gr_ref[...] = g_row

    ri = lax.broadcasted_iota(I32, (c, c), 0)
    ci = lax.broadcasted_iota(I32, (c, c), 1)
    causal = ri >= ci
    strict = ri > ci
    eye = (ri == ci).astype(F32)
    on = on_ref[...]
    n_ch = tm // c

    probs = []
    for j in range(hb):
        hd = hg * hb + j
        cs = slice(j * DN_HEAD_DIM, (j + 1) * DN_HEAD_DIM)
        q = q_all[:, cs]
        k = k_all[:, cs]
        q = q * lax.rsqrt(jnp.sum(q * q, axis=-1, keepdims=True) + EPS) * (DN_HEAD_DIM ** -0.5)
        k = k * lax.rsqrt(jnp.sum(k * k, axis=-1, keepdims=True) + EPS)
        gcum = jnp.sum(jnp.where(lane == hd + DN_HEADS, g_all, 0.0), axis=1, keepdims=True)
        beta = jnp.sum(jnp.where(lane == hd, sig_all, 0.0), axis=1, keepdims=True)
        grow_all = gr_ref[pl.ds(hd, 1), :]
        for ch in range(n_ch):
            rs = slice(ch * c, (ch + 1) * c)
            gcol = gcum[rs]
            kc = k[rs]
            kb = kc * beta[rs]
            probs.append(dict(
                j=j, ch=ch, qc=q[rs], kc=kc, kb=kb, vb=v_all[rs, cs] * beta[rs], gcol=gcol,
                eg=jnp.exp(gcol), glast=gcol[c - 1:c, :],
                decay=jnp.where(causal, jnp.exp(jnp.where(causal, gcol - grow_all[:, rs], 0.0)), 0.0)))
    for p in probs:
        p["prod"] = _dot_nt(jnp.concatenate([p["qc"], p["kb"]], axis=0).astype(BF16),
                            p["kc"].astype(BF16))
    for p in probs:
        p["qk"] = jnp.where(causal, p["prod"][0:c] * p["decay"], 0.0).astype(BF16)
        low = jnp.where(strict, p["prod"][c:2 * c] * p["decay"], 0.0)
        p["inv"] = eye - low
        p["pw"] = low.astype(BF16)
    n_sq = 1
    while 2 * n_sq < c:
        for p in probs:
            p["pw"] = _dot(p["pw"], p["pw"]).astype(BF16)
        for p in probs:
            p["inv"] = p["inv"] + _dot(p["inv"].astype(BF16), p["pw"])
        n_sq *= 2
    for p in probs:
        rhs = jnp.concatenate([p["vb"], p["kb"] * p["eg"]], axis=1).astype(BF16)
        sol = _dot(p["inv"].astype(BF16), rhs)
        p["u"] = sol[:, 0:DN_HEAD_DIM]
        w = sol[:, DN_HEAD_DIM:2 * DN_HEAD_DIM]
        p["wq"] = jnp.concatenate([w, p["qc"] * p["eg"]], axis=0).astype(BF16)
        p["k_dec"] = (p["kc"] * jnp.exp(p["glast"] - p["gcol"])).astype(BF16)
        p["g_end"] = jnp.exp(p["glast"])

    states = [st_ref[j] for j in range(hb)]
    for ch in range(n_ch):
        rs = slice(ch * c, (ch + 1) * c)
        cur = [probs[j * n_ch + ch] for j in range(hb)]
        wss = [_dot(p["wq"], states[j].astype(BF16)) for j, p in enumerate(cur)]
        v_nbs = [(p["u"] - ws[0:c]).astype(BF16) for p, ws in zip(cur, wss)]
        outs = [ws[c:2 * c] + _dot(p["qk"], v_nb) for p, ws, v_nb in zip(cur, wss, v_nbs)]
        states = [states[j] * p["g_end"] + _dot_tn(p["k_dec"], v_nb)
                  for j, (p, v_nb) in enumerate(zip(cur, v_nbs))]
        for j, o in enumerate(outs):
            cs = slice(j * DN_HEAD_DIM, (j + 1) * DN_HEAD_DIM)
            o_ref[rs, cs] = (_rms(o, on) * jax.nn.silu(gate_ref[rs, cs])).astype(BF16)
    for j in range(hb):
        st_ref[j] = states[j]


def _gated_deltanet(proj, ba_col, ba_row, conv_w, a_log, dt_bias, out_norm, bsz, tm, hb):
    t = proj.shape[0]
    ns = t // bsz // tm
    hh = DN_HEADS
    ng = hh // hb
    wb = hb * DN_HEAD_DIM
    q0 = 2 * D_LRU // wb
    pad = jnp.zeros((hh,), F32)
    al_c = jnp.concatenate([pad, a_log, jnp.zeros((LANES - 2 * hh,), F32)]).reshape(1, LANES)
    dt_c = jnp.concatenate([pad, dt_bias, jnp.zeros((LANES - 2 * hh,), F32)]).reshape(1, LANES)
    al_r = a_log.reshape(hh, 1)
    dt_r = dt_bias.reshape(hh, 1)
    tok = lambda col0: (lambda b, h, s: (b * ns + s, col0 + h))
    cw = lambda col0: (lambda b, h, s: (0, col0 + h))
    const = lambda b, h, s: (0, 0)
    return pl.pallas_call(
        functools.partial(_gdn_kernel, tm=tm, hb=hb),
        grid=(bsz, ng, ns),
        in_specs=[
            pl.BlockSpec((tm, wb), tok(q0)),
            pl.BlockSpec((tm, wb), tok(q0 + ng)),
            pl.BlockSpec((tm, wb), tok(q0 + 2 * ng)),
            pl.BlockSpec((tm, wb), tok(q0 + 3 * ng)),
            pl.BlockSpec((CONV_W, wb), cw(0)),
            pl.BlockSpec((CONV_W, wb), cw(ng)),
            pl.BlockSpec((CONV_W, wb), cw(2 * ng)),
            pl.BlockSpec((tm, LANES), lambda b, h, s: (b * ns + s, 0)),
            pl.BlockSpec((2 * hh, tm), lambda b, h, s: (0, b * ns + s)),
            pl.BlockSpec((1, LANES), const),
            pl.BlockSpec((1, LANES), const),
            pl.BlockSpec((hh, 1), const),
            pl.BlockSpec((hh, 1), const),
            pl.BlockSpec((1, DN_HEAD_DIM), const),
        ],
        out_specs=pl.BlockSpec((tm, wb), lambda b, h, s: (b * ns + s, h)),
        out_shape=jax.ShapeDtypeStruct((t, D_DN), BF16),
        scratch_shapes=[
            pltpu.VMEM((tm + SUBLANES, wb), F32),
            pltpu.VMEM((tm + SUBLANES, wb), F32),
            pltpu.VMEM((tm + SUBLANES, wb), F32),
            pltpu.VMEM((hb, DN_HEAD_DIM, DN_HEAD_DIM), F32),
            pltpu.VMEM((hh, tm), F32),
        ],
        compiler_params=_params("parallel", "parallel", "arbitrary"),
        name="gated_deltanet",
    )(proj, proj, proj, proj, conv_w, conv_w, conv_w, ba_col, ba_row,
      al_c, dt_c, al_r, dt_r, out_norm)


def _outproj_kernel(yl_ref, yd_ref, w_ref, x_ref, g_ref, xo_ref, h_ref, *, rows_as_tiles):
    acc = _dot(yl_ref[...], w_ref[0:D_LRU, :]) + _dot(yd_ref[...], w_ref[D_LRU:D_LRU + D_DN, :])
    xn = x_ref[...] + acc
    xo_ref[...] = xn
    hn = _rms(xn, g_ref[...])
    if rows_as_tiles:
        for cc in range(ROW_CHUNKS):
            h_ref[:, cc, :] = hn[:, cc * LANES:(cc + 1) * LANES]
    else:
        h_ref[...] = hn.astype(h_ref.dtype)


def _out_proj(y_lru, y_dn, w, x, gain, tm, rows_as_tiles):
    t, d = x.shape
    if rows_as_tiles:
        h_spec = pl.BlockSpec((tm, ROW_CHUNKS, LANES), lambda i: (i, 0, 0))
        h_shape = jax.ShapeDtypeStruct((t, ROW_CHUNKS, LANES), F32)
    else:
        h_spec = pl.BlockSpec((tm, d), lambda i: (i, 0))
        h_shape = jax.ShapeDtypeStruct((t, d), BF16)
    return pl.pallas_call(
        functools.partial(_outproj_kernel, rows_as_tiles=rows_as_tiles),
        grid=(t // tm,),
        in_specs=[
            pl.BlockSpec((tm, D_LRU), lambda i: (i, 0)),
            pl.BlockSpec((tm, D_DN), lambda i: (i, 0)),
            pl.BlockSpec((D_LRU + D_DN, d), lambda i: (0, 0)),
            pl.BlockSpec((tm, d), lambda i: (i, 0)),
            pl.BlockSpec((1, d), lambda i: (0, 0)),
        ],
        out_specs=[pl.BlockSpec((tm, d), lambda i: (i, 0)), h_spec],
        out_shape=[jax.ShapeDtypeStruct((t, d), F32), h_shape],
        compiler_params=_params("parallel"),
        name="out_proj",
    )(y_lru, y_dn, w, x, gain)


CAST_ROWS = 64


def _ffn_kernel(h_ref, x_ref, wg_ref, wu_ref, wd_ref, *rest, cast_blocks):
    n_cast = len(cast_blocks)
    src_refs, o_ref, dst_refs = rest[:n_cast], rest[n_cast], rest[n_cast + 1:]

    @pl.when(pl.program_id(1) == 0)
    def _():
        o_ref[...] = x_ref[...]

    h = h_ref[...]
    g = _dot(h, wg_ref[...])
    u = _dot(h, wu_ref[...])
    o_ref[...] += _dot((jax.nn.silu(g) * u).astype(BF16), wd_ref[...])

    step = pl.program_id(0) * pl.num_programs(1) + pl.program_id(1)
    for src, dst, nb in zip(src_refs, dst_refs, cast_blocks):
        @pl.when(step < nb)
        def _(src=src, dst=dst):
            dst[...] = src[...].astype(BF16)


def _ffn_dense(h, x, wg, wu, wd, tm, tf, to_cast=()):
    t, d = x.shape
    ff = wg.shape[1]
    nf = ff // tf
    cast_blocks = tuple(a.shape[0] // CAST_ROWS for a in to_cast)
    assert all(nb <= (t // tm) * nf for nb in cast_blocks)
    cast_spec = lambda a, nb: pl.BlockSpec(
        (CAST_ROWS, a.shape[1]), lambda i, f: (jnp.minimum(i * nf + f, nb - 1), 0))
    cast_specs = [cast_spec(a, nb) for a, nb in zip(to_cast, cast_blocks)]
    res = pl.pallas_call(
        functools.partial(_ffn_kernel, cast_blocks=cast_blocks),
        grid=(t // tm, nf),
        in_specs=[
            pl.BlockSpec((tm, d), lambda i, f: (i, 0)),
            pl.BlockSpec((tm, d), lambda i, f: (i, 0)),
            pl.BlockSpec((d, tf), lambda i, f: (0, f)),
            pl.BlockSpec((d, tf), lambda i, f: (0, f)),
            pl.BlockSpec((tf, d), lambda i, f: (f, 0)),
        ] + cast_specs,
        out_specs=[pl.BlockSpec((tm, d), lambda i, f: (i, 0))] + cast_specs,
        out_shape=[jax.ShapeDtypeStruct((t, d), F32)]
        + [jax.ShapeDtypeStruct(a.shape, BF16) for a in to_cast],
        compiler_params=_params("arbitrary", "arbitrary"),
        name="ffn_dense",
    )(h, x, wg, wu, wd, *to_cast)
    return res[0], tuple(res[1:])


R_E1, R_E2, R_W1, R_W2, R_RANK1, R_RANK2 = range(6)


def _router_kernel(x_ref, g_ref, rt_ref, mr_ref, mc_ref, cnt_ref, carry_ref, *, tm):
    @pl.when(pl.program_id(0) == 0)
    def _():
        carry_ref[...] = jnp.zeros_like(carry_ref)

    h = _rms(x_ref[...], g_ref[...])
    rt = rt_ref[...]
    h_hi = h.astype(BF16)
    h_lo = (h - h_hi.astype(F32)).astype(BF16)
    rt_hi = rt.astype(BF16)
    rt_lo = (rt - rt_hi.astype(F32)).astype(BF16)
    logits = _dot_nt(rt_hi, h_hi) + (_dot_nt(rt_hi, h_lo) + _dot_nt(rt_lo, h_hi))
    eidx = lax.broadcasted_iota(I32, (N_EXPERTS, tm), 0)
    m1 = jnp.max(logits, axis=0, keepdims=True)
    i1 = jnp.min(jnp.where(logits == m1, eidx, N_EXPERTS), axis=0, keepdims=True)
    rest = jnp.where(eidx == i1, -jnp.inf, logits)
    m2 = jnp.max(rest, axis=0, keepdims=True)
    i2 = jnp.min(jnp.where(rest == m2, eidx, N_EXPERTS), axis=0, keepdims=True)
    e2 = jnp.exp(m2 - m1)
    w1 = 1.0 / (1.0 + e2)
    w2 = e2 / (1.0 + e2)
    oh1 = eidx == i1
    oh2 = eidx == i2
    cnt = jnp.where(oh1 | oh2, 1.0, 0.0)
    ti = lax.broadcasted_iota(I32, (tm, tm), 0)
    tj = lax.broadcasted_iota(I32, (tm, tm), 1)
    upper = jnp.where(ti <= tj, 1.0, 0.0).astype(BF16)
    cum = _dot(cnt.astype(BF16), upper)
    before = cum - cnt + carry_ref[:, 0:1]
    rank1 = jnp.sum(jnp.where(oh1, before, 0.0), axis=0, keepdims=True)
    rank2 = jnp.sum(jnp.where(oh2, before, 0.0), axis=0, keepdims=True)
    total = carry_ref[:, 0:1] + cum[:, tm - 1:tm]
    carry_ref[...] = jnp.broadcast_to(total, carry_ref.shape)
    cnt_ref[...] = jnp.broadcast_to(total, cnt_ref.shape).astype(I32)
    rec = jnp.concatenate(
        [i1.astype(F32), i2.astype(F32), w1, w2, rank1, rank2,
         jnp.zeros((LANES - 6, tm), F32)], axis=0)
    mr_ref[...] = rec[0:SUBLANES, :]
    mc_ref[...] = rec.T


def _router(x, gain, router_t, tm):
    t, d = x.shape
    return pl.pallas_call(
        functools.partial(_router_kernel, tm=tm),
        grid=(t // tm,),
        in_specs=[
            pl.BlockSpec((tm, d), lambda i: (i, 0)),
            pl.BlockSpec((1, d), lambda i: (0, 0)),
            pl.BlockSpec((N_EXPERTS, d), lambda i: (0, 0)),
        ],
        out_specs=[
            pl.BlockSpec((SUBLANES, tm), lambda i: (0, i)),
            pl.BlockSpec((tm, LANES), lambda i: (i, 0)),
            pl.BlockSpec((N_EXPERTS, LANES), lambda i: (0, 0)),
        ],
        out_shape=[
            jax.ShapeDtypeStruct((SUBLANES, t), F32),
            jax.ShapeDtypeStruct((t, LANES), F32),
            jax.ShapeDtypeStruct((N_EXPERTS, LANES), I32),
        ],
        scratch_shapes=[pltpu.VMEM((N_EXPERTS, LANES), F32)],
        compiler_params=_params("arbitrary"),
        name="moe_router",
    )(x, gain, router_t)


DMA_UNROLL = 8


def _dispatch_kernel(d1_ref, d2_ref, zb_ref, h_ref, xs_ref, z_ref, sem, zsem, *, tm, tmb, n_zero):
    base = pl.program_id(0) * tm

    @pl.when(pl.program_id(0) == 0)
    def _():
        z_ref[...] = jnp.zeros_like(z_ref)
        for e in range(n_zero):
            fill = pltpu.make_async_copy(z_ref, xs_ref.at[pl.ds(zb_ref[e], tmb)], zsem)
            fill.start()
            fill.wait()

    def issue(g, carry):
        for u in range(DMA_UNROLL):
            r = g * DMA_UNROLL + u
            pltpu.make_async_copy(h_ref.at[r], xs_ref.at[d1_ref[base + r]], sem.at[0]).start()
            pltpu.make_async_copy(h_ref.at[r], xs_ref.at[d2_ref[base + r]], sem.at[1]).start()
        return carry

    lax.fori_loop(0, tm // DMA_UNROLL, issue, 0)
    pltpu.make_async_copy(h_ref, xs_ref.at[pl.ds(0, tm)], sem.at[0]).wait()
    pltpu.make_async_copy(h_ref, xs_ref.at[pl.ds(0, tm)], sem.at[1]).wait()


def _dispatch(dest1, dest2, zero_blocks, h_tiles, n_slots, tm, tmb):
    t = h_tiles.shape[0]
    return pl.pallas_call(
        functools.partial(_dispatch_kernel, tm=tm, tmb=tmb, n_zero=zero_blocks.shape[0]),
        grid_spec=pltpu.PrefetchScalarGridSpec(
            num_scalar_prefetch=3,
            grid=(t // tm,),
            in_specs=[pl.BlockSpec((tm, ROW_CHUNKS, LANES), lambda i, d1, d2, zb: (i, 0, 0))],
            out_specs=pl.BlockSpec(memory_space=pl.ANY),
            scratch_shapes=[
                pltpu.VMEM((tmb, ROW_CHUNKS, LANES), F32),
                pltpu.SemaphoreType.DMA((2,)),
                pltpu.SemaphoreType.DMA(()),
            ],
        ),
        out_shape=jax.ShapeDtypeStruct((n_slots, ROW_CHUNKS, LANES), F32),
        compiler_params=_params("arbitrary"),
        name="moe_dispatch",
    )(dest1, dest2, zero_blocks, h_tiles)


def _gmm_kernel(be_ref, nu_ref, xs_ref, wg_ref, wu_ref, wd_ref, y_ref, xb_ref, acc_ref):
    del be_ref
    f = pl.program_id(1)
    used = pl.program_id(0) < nu_ref[0]

    @pl.when(jnp.logical_and(jnp.logical_not(used), f == pl.num_programs(1) - 1))
    def _():
        y_ref[...] = jnp.zeros_like(y_ref)

    @pl.when(used)
    def _():
        @pl.when(f == 0)
        def _():
            xb_ref[...] = jnp.concatenate(
                [xs_ref[:, cc, :] for cc in range(ROW_CHUNKS)], axis=1).astype(BF16)
            acc_ref[...] = jnp.zeros_like(acc_ref)

        xb = xb_ref[...]
        g = _dot(xb, wg_ref[...])
        u = _dot(xb, wu_ref[...])
        acc_ref[...] += _dot((jax.nn.silu(g) * u).astype(BF16), wd_ref[...])

        @pl.when(f == pl.num_programs(1) - 1)
        def _():
            for cc in range(ROW_CHUNKS):
                y_ref[:, cc, :] = acc_ref[:, cc * LANES:(cc + 1) * LANES]


def _gmm(block_expert, n_used, xs, wg, wu, wd, tm, tf):
    n_slots = xs.shape[0]
    d = D_MODEL
    ff = wg.shape[2]
    nf = ff // tf

    def blk(b, nu):
        return jnp.minimum(b, nu[0] - 1)

    def fidx(b, f, nu):
        return jnp.where(b < nu[0], f, nf - 1)

    return pl.pallas_call(
        _gmm_kernel,
        grid_spec=pltpu.PrefetchScalarGridSpec(
            num_scalar_prefetch=2,
            grid=(n_slots // tm, nf),
            in_specs=[
                pl.BlockSpec((tm, ROW_CHUNKS, LANES), lambda b, f, be, nu: (blk(b, nu), 0, 0)),
                pl.BlockSpec((None, d, tf), lambda b, f, be, nu: (be[blk(b, nu)], 0, fidx(b, f, nu))),
                pl.BlockSpec((None, d, tf), lambda b, f, be, nu: (be[blk(b, nu)], 0, fidx(b, f, nu))),
                pl.BlockSpec((None, tf, d), lambda b, f, be, nu: (be[blk(b, nu)], fidx(b, f, nu), 0)),
            ],
            out_specs=pl.BlockSpec((tm, ROW_CHUNKS, LANES), lambda b, f, be, nu: (b, 0, 0)),
            scratch_shapes=[pltpu.VMEM((tm, d), BF16), pltpu.VMEM((tm, d), F32)],
        ),
        out_shape=jax.ShapeDtypeStruct((n_slots, ROW_CHUNKS, LANES), F32),
        compiler_params=_params("arbitrary", "arbitrary"),
        name="moe_gmm",
    )(block_expert, n_used, xs, wg, wu, wd)


def _combine_kernel(d1_ref, d2_ref, x_ref, mc_ref, y_ref, g_ref, o_ref, b1_ref, b2_ref, sem, *, tm):
    i = pl.program_id(0)
    slot = i % 2

    def gather(step, sl):
        base = step * tm

        def issue(g, carry):
            for u in range(DMA_UNROLL):
                r = g * DMA_UNROLL + u
                pltpu.make_async_copy(y_ref.at[d1_ref[base + r]], b1_ref.at[sl, r], sem.at[sl, 0]).start()
                pltpu.make_async_copy(y_ref.at[d2_ref[base + r]], b2_ref.at[sl, r], sem.at[sl, 1]).start()
            return carry

        lax.fori_loop(0, tm // DMA_UNROLL, issue, 0)

    @pl.when(i == 0)
    def _():
        gather(0, 0)

    @pl.when(i + 1 < pl.num_programs(0))
    def _():
        gather(i + 1, 1 - slot)

    pltpu.make_async_copy(y_ref.at[pl.ds(0, tm)], b1_ref.at[slot], sem.at[slot, 0]).wait()
    pltpu.make_async_copy(y_ref.at[pl.ds(0, tm)], b2_ref.at[slot], sem.at[slot, 1]).wait()
    w1 = mc_ref[:, R_W1:R_W1 + 1]
    w2 = mc_ref[:, R_W2:R_W2 + 1]
    xn = jnp.concatenate(
        [x_ref[:, cc * LANES:(cc + 1) * LANES]
         + (w1 * b1_ref[slot, :, cc, :] + w2 * b2_ref[slot, :, cc, :])
         for cc in range(ROW_CHUNKS)], axis=1)
    o_ref[...] = _rms(xn, g_ref[...])


def _combine(dest1, dest2, x, meta_col, y, gain, tm):
    t, d = x.shape
    return pl.pallas_call(
        functools.partial(_combine_kernel, tm=tm),
        grid_spec=pltpu.PrefetchScalarGridSpec(
            num_scalar_prefetch=2,
            grid=(t // tm,),
            in_specs=[
                pl.BlockSpec((tm, d), lambda i, d1, d2: (i, 0)),
                pl.BlockSpec((tm, LANES), lambda i, d1, d2: (i, 0)),
                pl.BlockSpec(memory_space=pl.ANY),
                pl.BlockSpec((1, d), lambda i, d1, d2: (0, 0)),
            ],
            out_specs=pl.BlockSpec((tm, d), lambda i, d1, d2: (i, 0)),
            scratch_shapes=[
                pltpu.VMEM((2, tm, ROW_CHUNKS, LANES), F32),
                pltpu.VMEM((2, tm, ROW_CHUNKS, LANES), F32),
                pltpu.SemaphoreType.DMA((2, 2)),
            ],
        ),
        out_shape=jax.ShapeDtypeStruct((t, d), F32),
        compiler_params=_params("arbitrary"),
        name="moe_combine",
    )(dest1, dest2, x, meta_col, y, gain)


def _tiles(t, seq):
    pick = lambda want, total: want if total % want == 0 else total
    return dict(
        proj_tm=pick(1024, t), proj_tn=1024,
        lru_tm=pick(256, seq),
        gdn_tm=pick(512, seq), gdn_hb=4,
        out_tm=pick(512, t),
        ffn_tm=pick(512, t), ffn_tf=512,
        router_tm=pick(512, t),
        dispatch_tm=pick(512, t),
        gmm_tm=512, gmm_tf=512,
        combine_tm=pick(256, t),
    )


def _moe(x, h_tiles, norm_ffn, router, wg, wu, wd, norm_final, tl):
    t = x.shape[0]
    tmb = tl["gmm_tm"]
    meta_row, meta_col, counts = _router(x, norm_ffn, router.T, tl["router_tm"])
    counts = counts[:, 0]
    padded = (counts + tmb - 1) // tmb * tmb
    pad_end = jnp.cumsum(padded)
    pad_start = pad_end - padded
    e1 = meta_row[R_E1].astype(I32)
    e2 = meta_row[R_E2].astype(I32)
    experts = jnp.arange(N_EXPERTS, dtype=I32)[:, None]
    start1 = jnp.sum(jnp.where(e1[None, :] == experts, pad_start[:, None], 0), axis=0)
    start2 = jnp.sum(jnp.where(e2[None, :] == experts, pad_start[:, None], 0), axis=0)
    dest1 = start1 + meta_row[R_RANK1].astype(I32)
    dest2 = start2 + meta_row[R_RANK2].astype(I32)
    n_slots = 2 * t + N_EXPERTS * tmb
    n_blocks = n_slots // tmb
    block_start = jnp.arange(n_blocks, dtype=I32) * tmb
    block_expert = jnp.minimum(
        jnp.sum((block_start[:, None] >= pad_end[None, :]).astype(I32), axis=1), N_EXPERTS - 1)
    n_used = (pad_end[-1] // tmb).astype(I32).reshape(1)
    zero_blocks = jnp.concatenate([
        jnp.maximum(pad_end - tmb, 0),
        jnp.minimum(pad_end[-1] + jnp.arange(N_EXPERTS, dtype=I32) * tmb, n_slots - tmb),
    ]).astype(I32)
    xs = _dispatch(dest1, dest2, zero_blocks, h_tiles, n_slots, tl["dispatch_tm"], tmb)
    y = _gmm(block_expert, n_used, xs, wg, wu, wd, tmb, tl["gmm_tf"])
    return _combine(dest1, dest2, x, meta_col, y, norm_final, tl["combine_tm"])


def kernel(x, norm_mix, w_in, conv_lru_w, conv_lru_b, lru_w_r, lru_b_r, lru_w_i, lru_b_i,
           lru_lambda, lru_out_norm, conv_qkv_w, dn_a_log, dn_dt_bias, dn_out_norm, w_out,
           norm_ffn, ffn_w_gate, ffn_w_up, ffn_w_down, moe_router, moe_w_gate, moe_w_up,
           moe_w_down, norm_final):
    bsz, seq, d = x.shape
    t = bsz * seq
    depth = w_in.shape[0]
    tl = _tiles(t, seq)
    row = lambda v: v.reshape(1, -1)
    xt = x.reshape(t, d)
    out = None
    moe_bf16 = {}
    for l in range(depth):
        is_moe = l % 2 == 1
        w_main = w_in[l, :, :D_PROJ].astype(BF16)
        w_small = w_in[l, :, D_PROJ:]
        w_small_p = jnp.pad(w_small, ((0, 0), (0, LANES - 2 * DN_HEADS))).astype(BF16)
        proj, ba_col, ba_row = _in_proj(xt, row(norm_mix[l]), w_main, w_small_p,
                                        w_small.T.astype(BF16), tl["proj_tm"], tl["proj_tn"])
        y_lru = _rg_lru(proj, conv_lru_w[l], row(conv_lru_b[l]), lru_w_r[l].astype(BF16),
                        row(lru_b_r[l]), lru_w_i[l].astype(BF16), row(lru_b_i[l]),
                        row(lru_lambda[l]), row(lru_out_norm[l]), bsz, tl["lru_tm"])
        y_dn = _gated_deltanet(proj, ba_col, ba_row, conv_qkv_w[l], dn_a_log[l], dn_dt_bias[l],
                               row(dn_out_norm[l]), bsz, tl["gdn_tm"], tl["gdn_hb"])
        xt, h2 = _out_proj(y_lru, y_dn, w_out[l].astype(BF16), xt, row(norm_ffn[l]),
                           tl["out_tm"], rows_as_tiles=is_moe)
        j = l // 2
        if not is_moe:
            nxt = moe_w_gate[j], moe_w_up[j], moe_w_down[j]
            steps = (t // tl["ffn_tm"]) * (ffn_w_gate.shape[2] // tl["ffn_tf"])
            flat = tuple(w.reshape(-1, w.shape[-1]) for w in nxt)
            ride = l + 1 < depth and all(
                a.shape[0] % CAST_ROWS == 0 and a.shape[0] // CAST_ROWS <= steps for a in flat)
            xt, cast = _ffn_dense(h2, xt, ffn_w_gate[j].astype(BF16), ffn_w_up[j].astype(BF16),
                                  ffn_w_down[j].astype(BF16), tl["ffn_tm"], tl["ffn_tf"],
                                  flat if ride else ())
            if ride:
                moe_bf16[j] = tuple(c.reshape(w.shape) for c, w in zip(cast, nxt))
        else:
            assert l == depth - 1, "the routed layer fuses the final RMSNorm"
            wg, wu, wd = moe_bf16.get(j) or (
                moe_w_gate[j].astype(BF16), moe_w_up[j].astype(BF16), moe_w_down[j].astype(BF16))
            out = _moe(xt, h2, row(norm_ffn[l]), moe_router[j], wg, wu, wd, row(norm_final), tl)
    return out.reshape(bsz, seq, d)
```

```python
import functools

import jax
import jax.numpy as jnp
from jax import lax
from jax.experimental import pallas as pl
from jax.experimental.pallas import tpu as pltpu

F32 = jnp.float32
BF16 = jnp.bfloat16
I32 = jnp.int32

D_MODEL = 2048
D_LRU = 1024
LRU_BLOCKS = 8
LRU_BLOCK_W = 128
LRU_C = 8.0
CONV_W = 4
D_DN = 1024
DN_HEAD_DIM = 128
DN_HEADS = 8
DN_CHUNK = 64
D_PROJ = 2 * D_LRU + 4 * D_DN
N_EXPERTS = 8
EPS = 1e-6

LANES = 128
SUBLANES = 8
ROW_CHUNKS = D_MODEL // LANES
VMEM_LIMIT = 56 * 1024 * 1024


def _params(*sem):
    return pltpu.CompilerParams(dimension_semantics=sem, vmem_limit_bytes=VMEM_LIMIT)


def _rms(x, gain):
    ms = jnp.mean(x * x, axis=-1, keepdims=True)
    return x * lax.rsqrt(ms + EPS) * gain


def _softplus(x):
    return jnp.maximum(x, 0.0) + jnp.log1p(jnp.exp(-jnp.abs(x)))


def _dot(a, b):
    return jnp.dot(a, b, preferred_element_type=F32)


def _dot_nt(a, b, precision=None):
    return lax.dot_general(a, b, (((1,), (1,)), ((), ())),
                           preferred_element_type=F32, precision=precision)


def _dot_tn(a, b):
    return lax.dot_general(a, b, (((0,), (0,)), ((), ())), preferred_element_type=F32)


def _inproj_kernel(x_ref, g_ref, w_ref, ws_ref, wst_ref, o_ref, oc_ref, or_ref, h_ref):
    @pl.when(pl.program_id(1) == 0)
    def _():
        h = _rms(x_ref[...], g_ref[...]).astype(BF16)
        h_ref[...] = h
        oc_ref[...] = _dot(h, ws_ref[...])
        or_ref[...] = _dot_nt(wst_ref[...], h)

    o_ref[...] = _dot(h_ref[...], w_ref[...])


def _in_proj(x, gain, w_main, layer, w_small, w_small_t, tm, tn):
    t, d = x.shape
    n = w_main.shape[2]
    return pl.pallas_call(
        _inproj_kernel,
        grid=(t // tm, n // tn),
        in_specs=[
            pl.BlockSpec((tm, d), lambda i, j: (i, 0)),
            pl.BlockSpec((1, d), lambda i, j: (0, 0)),
            pl.BlockSpec((None, d, tn), lambda i, j: (layer, 0, j)),
            pl.BlockSpec((d, LANES), lambda i, j: (0, 0)),
            pl.BlockSpec((2 * DN_HEADS, d), lambda i, j: (0, 0)),
        ],
        out_specs=[
            pl.BlockSpec((tm, tn), lambda i, j: (i, j)),
            pl.BlockSpec((tm, LANES), lambda i, j: (i, 0)),
            pl.BlockSpec((2 * DN_HEADS, tm), lambda i, j: (0, i)),
        ],
        out_shape=[
            jax.ShapeDtypeStruct((t, n), F32),
            jax.ShapeDtypeStruct((t, LANES), F32),
            jax.ShapeDtypeStruct((2 * DN_HEADS, t), F32),
        ],
        scratch_shapes=[pltpu.VMEM((tm, d), BF16)],
        compiler_params=_params("parallel", "arbitrary"),
        name="in_proj",
    )(x, gain, w_main, w_small, w_small_t)


def _lru_kernel(x_ref, gate_ref, cw_ref, cb_ref, wr_ref, br_ref, wi_ref, bi_ref, lam_ref,
                on_ref, o_ref, xs_ref, hc_ref, y_ref, *, tm):
    @pl.when(pl.program_id(1) == 0)
    def _():
        xs_ref[0:SUBLANES, :] = jnp.zeros((SUBLANES, D_LRU), F32)
        hc_ref[...] = jnp.zeros_like(hc_ref)

    xs_ref[SUBLANES:SUBLANES + tm, :] = x_ref[...]
    row = lax.broadcasted_iota(I32, (tm, LRU_BLOCK_W), 0)
    ssq = jnp.zeros((tm, 1), F32)
    for n in range(LRU_BLOCKS):
        sl = slice(n * LRU_BLOCK_W, (n + 1) * LRU_BLOCK_W)
        x_ext = xs_ref[:, sl]
        xc = pltpu.roll(x_ext, CONV_W - 1, 0)[SUBLANES:, :] * cw_ref[0:1, sl]
        for k in range(1, CONV_W - 1):
            xc = xc + pltpu.roll(x_ext, CONV_W - 1 - k, 0)[SUBLANES:, :] * cw_ref[k:k + 1, sl]
        xc = xc + x_ext[SUBLANES:, :] * cw_ref[CONV_W - 1:CONV_W, sl]
        xc = xc + cb_ref[:, sl]
        xb = xc.astype(BF16)
        r = jax.nn.sigmoid(_dot(xb, wr_ref[n]) + br_ref[:, sl])
        i = jax.nn.sigmoid(_dot(xb, wi_ref[n]) + bi_ref[:, sl])
        log_a = -LRU_C * r * _softplus(-lam_ref[:, sl])
        a = jnp.exp(log_a)
        b = jnp.sqrt(-jnp.tanh(log_a) * (1.0 + a * a)) * (i * xc)
        d = 1
        while d < tm:
            keep = row >= d
            b = jnp.where(keep, a * pltpu.roll(b, d, 0) + b, b)
            a = jnp.where(keep, a * pltpu.roll(a, d, 0), a)
            d *= 2
        h = b + a * hc_ref[0:1, sl]
        hc_ref[0:1, sl] = h[tm - 1:tm, :]
        y = h * jax.nn.gelu(gate_ref[:, sl], approximate=True)
        y_ref[:, sl] = y
        ssq = ssq + jnp.sum(y * y, axis=-1, keepdims=True)
    inv = lax.rsqrt(ssq / D_LRU + EPS)
    o_ref[...] = (y_ref[...] * inv * on_ref[...]).astype(BF16)
    xs_ref[0:SUBLANES, :] = x_ref[tm - SUBLANES:tm, :]


def _rg_lru(proj, conv_w, conv_b, w_r, b_r, w_i, b_i, lam, out_norm, bsz, tm):
    t = proj.shape[0]
    ns = t // bsz // tm
    row = lambda b, s: (0, 0)
    return pl.pallas_call(
        functools.partial(_lru_kernel, tm=tm),
        grid=(bsz, ns),
        in_specs=[
            pl.BlockSpec((tm, D_LRU), lambda b, s: (b * ns + s, 0)),
            pl.BlockSpec((tm, D_LRU), lambda b, s: (b * ns + s, 1)),
            pl.BlockSpec((CONV_W, D_LRU), row),
            pl.BlockSpec((1, D_LRU), row),
            pl.BlockSpec((LRU_BLOCKS, LRU_BLOCK_W, LRU_BLOCK_W), lambda b, s: (0, 0, 0)),
            pl.BlockSpec((1, D_LRU), row),
            pl.BlockSpec((LRU_BLOCKS, LRU_BLOCK_W, LRU_BLOCK_W), lambda b, s: (0, 0, 0)),
            pl.BlockSpec((1, D_LRU), row),
            pl.BlockSpec((1, D_LRU), row),
            pl.BlockSpec((1, D_LRU), row),
        ],
        out_specs=pl.BlockSpec((tm, D_LRU), lambda b, s: (b * ns + s, 0)),
        out_shape=jax.ShapeDtypeStruct((t, D_LRU), BF16),
        scratch_shapes=[
            pltpu.VMEM((tm + SUBLANES, D_LRU), F32),
            pltpu.VMEM((SUBLANES, D_LRU), F32),
            pltpu.VMEM((tm, D_LRU), F32),
        ],
        compiler_params=_params("parallel", "arbitrary"),
        name="rg_lru",
    )(proj, proj, conv_w, conv_b, w_r, b_r, w_i, b_i, lam, out_norm)


def _gdn_kernel(q_ref, k_ref, v_ref, gate_ref, cwq_ref, cwk_ref, cwv_ref, bac_ref, bar_ref,
                alc_ref, dtc_ref, alr_ref, dtr_ref, on_ref, o_ref,
                qs_ref, ks_ref, vs_ref, st_ref, gr_ref, *, tm, hb):
    hg = pl.program_id(1)
    c = DN_CHUNK

    @pl.when(pl.program_id(2) == 0)
    def _():
        zeros = jnp.zeros((SUBLANES, hb * DN_HEAD_DIM), F32)
        qs_ref[0:SUBLANES, :] = zeros
        ks_ref[0:SUBLANES, :] = zeros
        vs_ref[0:SUBLANES, :] = zeros
        st_ref[...] = jnp.zeros_like(st_ref)

    def conv_silu(x_ref, xs_ref, cw_ref):
        xs_ref[SUBLANES:SUBLANES + tm, :] = x_ref[...]
        x_ext = xs_ref[...]
        y = pltpu.roll(x_ext, CONV_W - 1, 0)[SUBLANES:, :] * cw_ref[0:1, :]
        for kk in range(1, CONV_W - 1):
            y = y + pltpu.roll(x_ext, CONV_W - 1 - kk, 0)[SUBLANES:, :] * cw_ref[kk:kk + 1, :]
        y = y + x_ext[SUBLANES:, :] * cw_ref[CONV_W - 1:CONV_W, :]
        xs_ref[0:SUBLANES, :] = x_ref[tm - SUBLANES:tm, :]
        return jax.nn.silu(y)

    q_all = conv_silu(q_ref, qs_ref, cwq_ref)
    k_all = conv_silu(k_ref, ks_ref, cwk_ref)
    v_all = conv_silu(v_ref, vs_ref, cwv_ref)

    bac = bac_ref[...]
    lane = lax.broadcasted_iota(I32, (tm, LANES), 1)
    rowc = lax.broadcasted_iota(I32, (tm, LANES), 0) % c
    g_all = -jnp.exp(alc_ref[...]) * _softplus(bac + dtc_ref[...])
    d = 1
    while d < c:
        g_all = g_all + jnp.where(rowc >= d, pltpu.roll(g_all, d, 0), 0.0)
        d *= 2
    sig_all = jax.nn.sigmoid(bac)

    g_row = -jnp.exp(alr_ref[...]) * _softplus(bar_ref[DN_HEADS:2 * DN_HEADS, :] + dtr_ref[...])
    lanec = lax.broadcasted_iota(I32, (DN_HEADS, tm), 1) % c
    d = 1
    while d < c:
        g_row = g_row + jnp.where(lanec >= d, pltpu.roll(g_row, d, 1), 0.0)
        d *= 2
    gr_ref[...] = g_row

    ri = lax.broadcasted_iota(I32, (c, c), 0)
    ci = lax.broadcasted_iota(I32, (c, c), 1)
    causal = ri >= ci
    strict = ri > ci
    eye = (ri == ci).astype(F32)
    on = on_ref[...]
    n_ch = tm // c

    probs = []
    for j in range(hb):
        hd = hg * hb + j
        cs = slice(j * DN_HEAD_DIM, (j + 1) * DN_HEAD_DIM)
        q = q_all[:, cs]
        k = k_all[:, cs]
        q = q * lax.rsqrt(jnp.sum(q * q, axis=-1, keepdims=True) + EPS) * (DN_HEAD_DIM ** -0.5)
        k = k * lax.rsqrt(jnp.sum(k * k, axis=-1, keepdims=True) + EPS)
        gcum = jnp.sum(jnp.where(lane == hd + DN_HEADS, g_all, 0.0), axis=1, keepdims=True)
        beta = jnp.sum(jnp.where(lane == hd, sig_all, 0.0), axis=1, keepdims=True)
        grow_all = gr_ref[pl.ds(hd, 1), :]
        for ch in range(n_ch):
            rs = slice(ch * c, (ch + 1) * c)
            gcol = gcum[rs]
            kc = k[rs]
            kb = kc * beta[rs]
            probs.append(dict(
                j=j, ch=ch, qc=q[rs], kc=kc, kb=kb, vb=v_all[rs, cs] * beta[rs], gcol=gcol,
                eg=jnp.exp(gcol), glast=gcol[c - 1:c, :],
                decay=jnp.where(causal, jnp.exp(jnp.where(causal, gcol - grow_all[:, rs], 0.0)), 0.0)))
    for p in probs:
        p["prod"] = _dot_nt(jnp.concatenate([p["qc"], p["kb"]], axis=0).astype(BF16),
                            p["kc"].astype(BF16))
    for p in probs:
        p["qk"] = jnp.where(causal, p["prod"][0:c] * p["decay"], 0.0).astype(BF16)
        low = jnp.where(strict, p["prod"][c:2 * c] * p["decay"], 0.0)
        p["inv"] = eye - low
        p["pw"] = low.astype(BF16)
    n_sq = 1
    while 2 * n_sq < c:
        for p in probs:
            p["pw"] = _dot(p["pw"], p["pw"]).astype(BF16)
        for p in probs:
            p["inv"] = p["inv"] + _dot(p["inv"].astype(BF16), p["pw"])
        n_sq *= 2
    for p in probs:
        rhs = jnp.concatenate([p["vb"], p["kb"] * p["eg"]], axis=1).astype(BF16)
        sol = _dot(p["inv"].astype(BF16), rhs)
        p["u"] = sol[:, 0:DN_HEAD_DIM]
        w = sol[:, DN_HEAD_DIM:2 * DN_HEAD_DIM]
        p["wq"] = jnp.concatenate([w, p["qc"] * p["eg"]], axis=0).astype(BF16)
        p["k_dec"] = (p["kc"] * jnp.exp(p["glast"] - p["gcol"])).astype(BF16)
        p["g_end"] = jnp.exp(p["glast"])

    states = [st_ref[j] for j in range(hb)]
    for ch in range(n_ch):
        rs = slice(ch * c, (ch + 1) * c)
        cur = [probs[j * n_ch + ch] for j in range(hb)]
        wss = [_dot(p["wq"], states[j].astype(BF16)) for j, p in enumerate(cur)]
        v_nbs = [(p["u"] - ws[0:c]).astype(BF16) for p, ws in zip(cur, wss)]
        outs = [ws[c:2 * c] + _dot(p["qk"], v_nb) for p, ws, v_nb in zip(cur, wss, v_nbs)]
        states = [states[j] * p["g_end"] + _dot_tn(p["k_dec"], v_nb)
                  for j, (p, v_nb) in enumerate(zip(cur, v_nbs))]
        for j, o in enumerate(outs):
            cs = slice(j * DN_HEAD_DIM, (j + 1) * DN_HEAD_DIM)
            o_ref[rs, cs] = (_rms(o, on) * jax.nn.silu(gate_ref[rs, cs])).astype(BF16)
    for j in range(hb):
        st_ref[j] = states[j]


def _gated_deltanet(proj, ba_col, ba_row, conv_w, a_log, dt_bias, out_norm, bsz, tm, hb):
    t = proj.shape[0]
    ns = t // bsz // tm
    hh = DN_HEADS
    ng = hh // hb
    wb = hb * DN_HEAD_DIM
    q0 = 2 * D_LRU // wb
    pad = jnp.zeros((hh,), F32)
    al_c = jnp.concatenate([pad, a_log, jnp.zeros((LANES - 2 * hh,), F32)]).reshape(1, LANES)
    dt_c = jnp.concatenate([pad, dt_bias, jnp.zeros((LANES - 2 * hh,), F32)]).reshape(1, LANES)
    al_r = a_log.reshape(hh, 1)
    dt_r = dt_bias.reshape(hh, 1)
    tok = lambda col0: (lambda b, h, s: (b * ns + s, col0 + h))
    cw = lambda col0: (lambda b, h, s: (0, col0 + h))
    const = lambda b, h, s: (0, 0)
    return pl.pallas_call(
        functools.partial(_gdn_kernel, tm=tm, hb=hb),
        grid=(bsz, ng, ns),
        in_specs=[
            pl.BlockSpec((tm, wb), tok(q0)),
            pl.BlockSpec((tm, wb), tok(q0 + ng)),
            pl.BlockSpec((tm, wb), tok(q0 + 2 * ng)),
            pl.BlockSpec((tm, wb), tok(q0 + 3 * ng)),
            pl.BlockSpec((CONV_W, wb), cw(0)),
            pl.BlockSpec((CONV_W, wb), cw(ng)),
            pl.BlockSpec((CONV_W, wb), cw(2 * ng)),
            pl.BlockSpec((tm, LANES), lambda b, h, s: (b * ns + s, 0)),
            pl.BlockSpec((2 * hh, tm), lambda b, h, s: (0, b * ns + s)),
            pl.BlockSpec((1, LANES), const),
            pl.BlockSpec((1, LANES), const),
            pl.BlockSpec((hh, 1), const),
            pl.BlockSpec((hh, 1), const),
            pl.BlockSpec((1, DN_HEAD_DIM), const),
        ],
        out_specs=pl.BlockSpec((tm, wb), lambda b, h, s: (b * ns + s, h)),
        out_shape=jax.ShapeDtypeStruct((t, D_DN), BF16),
        scratch_shapes=[
            pltpu.VMEM((tm + SUBLANES, wb), F32),
            pltpu.VMEM((tm + SUBLANES, wb), F32),
            pltpu.VMEM((tm + SUBLANES, wb), F32),
            pltpu.VMEM((hb, DN_HEAD_DIM, DN_HEAD_DIM), F32),
            pltpu.VMEM((hh, tm), F32),
        ],
        compiler_params=_params("parallel", "parallel", "arbitrary"),
        name="gated_deltanet",
    )(proj, proj, proj, proj, conv_w, conv_w, conv_w, ba_col, ba_row,
      al_c, dt_c, al_r, dt_r, out_norm)


def _outproj_kernel(yl_ref, yd_ref, w_ref, x_ref, g_ref, xo_ref, h_ref, *, rows_as_tiles):
    acc = _dot(yl_ref[...], w_ref[0:D_LRU, :]) + _dot(yd_ref[...], w_ref[D_LRU:D_LRU + D_DN, :])
    xn = x_ref[...] + acc
    xo_ref[...] = xn
    hn = _rms(xn, g_ref[...])
    if rows_as_tiles:
        for cc in range(ROW_CHUNKS):
            h_ref[:, cc, :] = hn[:, cc * LANES:(cc + 1) * LANES]
    else:
        h_ref[...] = hn.astype(h_ref.dtype)


def _out_proj(y_lru, y_dn, w, x, gain, tm, rows_as_tiles):
    t, d = x.shape
    if rows_as_tiles:
        h_spec = pl.BlockSpec((tm, ROW_CHUNKS, LANES), lambda i: (i, 0, 0))
        h_shape = jax.ShapeDtypeStruct((t, ROW_CHUNKS, LANES), F32)
    else:
        h_spec = pl.BlockSpec((tm, d), lambda i: (i, 0))
        h_shape = jax.ShapeDtypeStruct((t, d), BF16)
    return pl.pallas_call(
        functools.partial(_outproj_kernel, rows_as_tiles=rows_as_tiles),
        grid=(t // tm,),
        in_specs=[
            pl.BlockSpec((tm, D_LRU), lambda i: (i, 0)),
            pl.BlockSpec((tm, D_DN), lambda i: (i, 0)),
            pl.BlockSpec((D_LRU + D_DN, d), lambda i: (0, 0)),
            pl.BlockSpec((tm, d), lambda i: (i, 0)),
            pl.BlockSpec((1, d), lambda i: (0, 0)),
        ],
        out_specs=[pl.BlockSpec((tm, d), lambda i: (i, 0)), h_spec],
        out_shape=[jax.ShapeDtypeStruct((t, d), F32), h_shape],
        compiler_params=_params("parallel"),
        name="out_proj",
    )(y_lru, y_dn, w, x, gain)


CAST_ROWS = 64


def _ffn_kernel(h_ref, x_ref, wg_ref, wu_ref, wd_ref, *rest, cast_blocks):
    n_cast = len(cast_blocks)
    src_refs, o_ref, dst_refs = rest[:n_cast], rest[n_cast], rest[n_cast + 1:]

    @pl.when(pl.program_id(1) == 0)
    def _():
        o_ref[...] = x_ref[...]

    h = h_ref[...]
    g = _dot(h, wg_ref[...])
    u = _dot(h, wu_ref[...])
    o_ref[...] += _dot((jax.nn.silu(g) * u).astype(BF16), wd_ref[...])

    step = pl.program_id(0) * pl.num_programs(1) + pl.program_id(1)
    for src, dst, nb in zip(src_refs, dst_refs, cast_blocks):
        @pl.when(step < nb)
        def _(src=src, dst=dst):
            dst[...] = src[...].astype(BF16)


def _ffn_dense(h, x, wg, wu, wd, tm, tf, to_cast=()):
    t, d = x.shape
    ff = wg.shape[1]
    nf = ff // tf
    cast_blocks = tuple(a.shape[0] // CAST_ROWS for a in to_cast)
    assert all(nb <= (t // tm) * nf for nb in cast_blocks)
    cast_spec = lambda a, nb: pl.BlockSpec(
        (CAST_ROWS, a.shape[1]), lambda i, f: (jnp.minimum(i * nf + f, nb - 1), 0))
    cast_specs = [cast_spec(a, nb) for a, nb in zip(to_cast, cast_blocks)]
    res = pl.pallas_call(
        functools.partial(_ffn_kernel, cast_blocks=cast_blocks),
        grid=(t // tm, nf),
        in_specs=[
            pl.BlockSpec((tm, d), lambda i, f: (i, 0)),
            pl.BlockSpec((tm, d), lambda i, f: (i, 0)),
            pl.BlockSpec((d, tf), lambda i, f: (0, f)),
            pl.BlockSpec((d, tf), lambda i, f: (0, f)),
            pl.BlockSpec((tf, d), lambda i, f: (f, 0)),
        ] + cast_specs,
        out_specs=[pl.BlockSpec((tm, d), lambda i, f: (i, 0))] + cast_specs,
        out_shape=[jax.ShapeDtypeStruct((t, d), F32)]
        + [jax.ShapeDtypeStruct(a.shape, BF16) for a in to_cast],
        compiler_params=_params("arbitrary", "arbitrary"),
        name="ffn_dense",
    )(h, x, wg, wu, wd, *to_cast)
    return res[0], tuple(res[1:])


R_E1, R_E2, R_W1, R_W2, R_RANK1, R_RANK2 = range(6)


def _router_kernel(x_ref, g_ref, rt_ref, mr_ref, mc_ref, cnt_ref, carry_ref, *, tm):
    @pl.when(pl.program_id(0) == 0)
    def _():
        carry_ref[...] = jnp.zeros_like(carry_ref)

    h = _rms(x_ref[...], g_ref[...])
    rt = rt_ref[...]
    h_hi = h.astype(BF16)
    h_lo = (h - h_hi.astype(F32)).astype(BF16)
    rt_hi = rt.astype(BF16)
    rt_lo = (rt - rt_hi.astype(F32)).astype(BF16)
    logits = _dot_nt(rt_hi, h_hi) + (_dot_nt(rt_hi, h_lo) + _dot_nt(rt_lo, h_hi))
    eidx = lax.broadcasted_iota(I32, (N_EXPERTS, tm), 0)
    m1 = jnp.max(logits, axis=0, keepdims=True)
    i1 = jnp.min(jnp.where(logits == m1, eidx, N_EXPERTS), axis=0, keepdims=True)
    rest = jnp.where(eidx == i1, -jnp.inf, logits)
    m2 = jnp.max(rest, axis=0, keepdims=True)
    i2 = jnp.min(jnp.where(rest == m2, eidx, N_EXPERTS), axis=0, keepdims=True)
    e2 = jnp.exp(m2 - m1)
    w1 = 1.0 / (1.0 + e2)
    w2 = e2 / (1.0 + e2)
    oh1 = eidx == i1
    oh2 = eidx == i2
    cnt = jnp.where(oh1 | oh2, 1.0, 0.0)
    ti = lax.broadcasted_iota(I32, (tm, tm), 0)
    tj = lax.broadcasted_iota(I32, (tm, tm), 1)
    upper = jnp.where(ti <= tj, 1.0, 0.0).astype(BF16)
    cum = _dot(cnt.astype(BF16), upper)
    before = cum - cnt + carry_ref[:, 0:1]
    rank1 = jnp.sum(jnp.where(oh1, before, 0.0), axis=0, keepdims=True)
    rank2 = jnp.sum(jnp.where(oh2, before, 0.0), axis=0, keepdims=True)
    total = carry_ref[:, 0:1] + cum[:, tm - 1:tm]
    carry_ref[...] = jnp.broadcast_to(total, carry_ref.shape)
    cnt_ref[...] = jnp.broadcast_to(total, cnt_ref.shape).astype(I32)
    rec = jnp.concatenate(
        [i1.astype(F32), i2.astype(F32), w1, w2, rank1, rank2,
         jnp.zeros((LANES - 6, tm), F32)], axis=0)
    mr_ref[...] = rec[0:SUBLANES, :]
    mc_ref[...] = rec.T


def _router(x, gain, router_t, tm):
    t, d = x.shape
    return pl.pallas_call(
        functools.partial(_router_kernel, tm=tm),
        grid=(t // tm,),
        in_specs=[
            pl.BlockSpec((tm, d), lambda i: (i, 0)),
            pl.BlockSpec((1, d), lambda i: (0, 0)),
            pl.BlockSpec((N_EXPERTS, d), lambda i: (0, 0)),
        ],
        out_specs=[
            pl.BlockSpec((SUBLANES, tm), lambda i: (0, i)),
            pl.BlockSpec((tm, LANES), lambda i: (i, 0)),
            pl.BlockSpec((N_EXPERTS, LANES), lambda i: (0, 0)),
        ],
        out_shape=[
            jax.ShapeDtypeStruct((SUBLANES, t), F32),
            jax.ShapeDtypeStruct((t, LANES), F32),
            jax.ShapeDtypeStruct((N_EXPERTS, LANES), I32),
        ],
        scratch_shapes=[pltpu.VMEM((N_EXPERTS, LANES), F32)],
        compiler_params=_params("arbitrary"),
        name="moe_router",
    )(x, gain, router_t)


DMA_UNROLL = 8


def _dispatch_kernel(d1_ref, d2_ref, zb_ref, h_ref, xs_ref, z_ref, sem, zsem, *, tm, tmb, n_zero):
    base = pl.program_id(0) * tm

    @pl.when(pl.program_id(0) == 0)
    def _():
        z_ref[...] = jnp.zeros_like(z_ref)
        for e in range(n_zero):
            fill = pltpu.make_async_copy(z_ref, xs_ref.at[pl.ds(zb_ref[e], tmb)], zsem)
            fill.start()
            fill.wait()

    def issue(g, carry):
        for u in range(DMA_UNROLL):
            r = g * DMA_UNROLL + u
            pltpu.make_async_copy(h_ref.at[r], xs_ref.at[d1_ref[base + r]], sem.at[0]).start()
            pltpu.make_async_copy(h_ref.at[r], xs_ref.at[d2_ref[base + r]], sem.at[1]).start()
        return carry

    lax.fori_loop(0, tm // DMA_UNROLL, issue, 0)
    pltpu.make_async_copy(h_ref, xs_ref.at[pl.ds(0, tm)], sem.at[0]).wait()
    pltpu.make_async_copy(h_ref, xs_ref.at[pl.ds(0, tm)], sem.at[1]).wait()


def _dispatch(dest1, dest2, zero_blocks, h_tiles, n_slots, tm, tmb):
    t = h_tiles.shape[0]
    return pl.pallas_call(
        functools.partial(_dispatch_kernel, tm=tm, tmb=tmb, n_zero=zero_blocks.shape[0]),
        grid_spec=pltpu.PrefetchScalarGridSpec(
            num_scalar_prefetch=3,
            grid=(t // tm,),
            in_specs=[pl.BlockSpec((tm, ROW_CHUNKS, LANES), lambda i, d1, d2, zb: (i, 0, 0))],
            out_specs=pl.BlockSpec(memory_space=pl.ANY),
            scratch_shapes=[
                pltpu.VMEM((tmb, ROW_CHUNKS, LANES), F32),
                pltpu.SemaphoreType.DMA((2,)),
                pltpu.SemaphoreType.DMA(()),
            ],
        ),
        out_shape=jax.ShapeDtypeStruct((n_slots, ROW_CHUNKS, LANES), F32),
        compiler_params=_params("arbitrary"),
        name="moe_dispatch",
    )(dest1, dest2, zero_blocks, h_tiles)


def _gmm_kernel(be_ref, nu_ref, xs_ref, wg_ref, wu_ref, wd_ref, y_ref, xb_ref, acc_ref):
    del be_ref
    f = pl.program_id(1)
    used = pl.program_id(0) < nu_ref[0]

    @pl.when(jnp.logical_and(jnp.logical_not(used), f == pl.num_programs(1) - 1))
    def _():
        y_ref[...] = jnp.zeros_like(y_ref)

    @pl.when(used)
    def _():
        @pl.when(f == 0)
        def _():
            xb_ref[...] = jnp.concatenate(
                [xs_ref[:, cc, :] for cc in range(ROW_CHUNKS)], axis=1).astype(BF16)
            acc_ref[...] = jnp.zeros_like(acc_ref)

        xb = xb_ref[...]
        g = _dot(xb, wg_ref[...])
        u = _dot(xb, wu_ref[...])
        acc_ref[...] += _dot((jax.nn.silu(g) * u).astype(BF16), wd_ref[...])

        @pl.when(f == pl.num_programs(1) - 1)
        def _():
            for cc in range(ROW_CHUNKS):
                y_ref[:, cc, :] = acc_ref[:, cc * LANES:(cc + 1) * LANES]


def _gmm(block_expert, n_used, xs, wg, wu, wd, tm, tf):
    n_slots = xs.shape[0]
    d = D_MODEL
    ff = wg.shape[2]
    nf = ff // tf

    def blk(b, nu):
        return jnp.minimum(b, nu[0] - 1)

    def fidx(b, f, nu):
        return jnp.where(b < nu[0], f, nf - 1)

    return pl.pallas_call(
        _gmm_kernel,
        grid_spec=pltpu.PrefetchScalarGridSpec(
            num_scalar_prefetch=2,
            grid=(n_slots // tm, nf),
            in_specs=[
                pl.BlockSpec((tm, ROW_CHUNKS, LANES), lambda b, f, be, nu: (blk(b, nu), 0, 0)),
                pl.BlockSpec((None, d, tf), lambda b, f, be, nu: (be[blk(b, nu)], 0, fidx(b, f, nu))),
                pl.BlockSpec((None, d, tf), lambda b, f, be, nu: (be[blk(b, nu)], 0, fidx(b, f, nu))),
                pl.BlockSpec((None, tf, d), lambda b, f, be, nu: (be[blk(b, nu)], fidx(b, f, nu), 0)),
            ],
            out_specs=pl.BlockSpec((tm, ROW_CHUNKS, LANES), lambda b, f, be, nu: (b, 0, 0)),
            scratch_shapes=[pltpu.VMEM((tm, d), BF16), pltpu.VMEM((tm, d), F32)],
        ),
        out_shape=jax.ShapeDtypeStruct((n_slots, ROW_CHUNKS, LANES), F32),
        compiler_params=_params("arbitrary", "arbitrary"),
        name="moe_gmm",
    )(block_expert, n_used, xs, wg, wu, wd)


def _combine_kernel(d1_ref, d2_ref, x_ref, mc_ref, y_ref, g_ref, o_ref, b1_ref, b2_ref, sem, *, tm):
    i = pl.program_id(0)
    slot = i % 2

    def gather(step, sl):
        base = step * tm

        def issue(g, carry):
            for u in range(DMA_UNROLL):
                r = g * DMA_UNROLL + u
                pltpu.make_async_copy(y_ref.at[d1_ref[base + r]], b1_ref.at[sl, r], sem.at[sl, 0]).start()
                pltpu.make_async_copy(y_ref.at[d2_ref[base + r]], b2_ref.at[sl, r], sem.at[sl, 1]).start()
            return carry

        lax.fori_loop(0, tm // DMA_UNROLL, issue, 0)

    @pl.when(i == 0)
    def _():
        gather(0, 0)

    @pl.when(i + 1 < pl.num_programs(0))
    def _():
        gather(i + 1, 1 - slot)

    pltpu.make_async_copy(y_ref.at[pl.ds(0, tm)], b1_ref.at[slot], sem.at[slot, 0]).wait()
    pltpu.make_async_copy(y_ref.at[pl.ds(0, tm)], b2_ref.at[slot], sem.at[slot, 1]).wait()
    w1 = mc_ref[:, R_W1:R_W1 + 1]
    w2 = mc_ref[:, R_W2:R_W2 + 1]
    xn = jnp.concatenate(
        [x_ref[:, cc * LANES:(cc + 1) * LANES]
         + (w1 * b1_ref[slot, :, cc, :] + w2 * b2_ref[slot, :, cc, :])
         for cc in range(ROW_CHUNKS)], axis=1)
    o_ref[...] = _rms(xn, g_ref[...])


def _combine(dest1, dest2, x, meta_col, y, gain, tm):
    t, d = x.shape
    return pl.pallas_call(
        functools.partial(_combine_kernel, tm=tm),
        grid_spec=pltpu.PrefetchScalarGridSpec(
            num_scalar_prefetch=2,
            grid=(t // tm,),
            in_specs=[
                pl.BlockSpec((tm, d), lambda i, d1, d2: (i, 0)),
                pl.BlockSpec((tm, LANES), lambda i, d1, d2: (i, 0)),
                pl.BlockSpec(memory_space=pl.ANY),
                pl.BlockSpec((1, d), lambda i, d1, d2: (0, 0)),
            ],
            out_specs=pl.BlockSpec((tm, d), lambda i, d1, d2: (i, 0)),
            scratch_shapes=[
                pltpu.VMEM((2, tm, ROW_CHUNKS, LANES), F32),
                pltpu.VMEM((2, tm, ROW_CHUNKS, LANES), F32),
                pltpu.SemaphoreType.DMA((2, 2)),
            ],
        ),
        out_shape=jax.ShapeDtypeStruct((t, d), F32),
        compiler_params=_params("arbitrary"),
        name="moe_combine",
    )(dest1, dest2, x, meta_col, y, gain)


def _tiles(t, seq):
    pick = lambda want, total: want if total % want == 0 else total
    return dict(
        proj_tm=pick(1024, t), proj_tn=1024,
        lru_tm=pick(256, seq),
        gdn_tm=pick(512, seq), gdn_hb=4,
        out_tm=pick(512, t),
        ffn_tm=pick(512, t), ffn_tf=512,
        router_tm=pick(512, t),
        dispatch_tm=pick(512, t),
        gmm_tm=512, gmm_tf=512,
        combine_tm=pick(512, t),
    )


def _moe(x, h_tiles, norm_ffn, router, wg, wu, wd, norm_final, tl):
    t = x.shape[0]
    tmb = tl["gmm_tm"]
    meta_row, meta_col, counts = _router(x, norm_ffn, router.T, tl["router_tm"])
    counts = counts[:, 0]
    padded = (counts + tmb - 1) // tmb * tmb
    pad_end = jnp.cumsum(padded)
    pad_start = pad_end - padded
    e1 = meta_row[R_E1].astype(I32)
    e2 = meta_row[R_E2].astype(I32)
    experts = jnp.arange(N_EXPERTS, dtype=I32)[:, None]
    start1 = jnp.sum(jnp.where(e1[None, :] == experts, pad_start[:, None], 0), axis=0)
    start2 = jnp.sum(jnp.where(e2[None, :] == experts, pad_start[:, None], 0), axis=0)
    dest1 = start1 + meta_row[R_RANK1].astype(I32)
    dest2 = start2 + meta_row[R_RANK2].astype(I32)
    n_slots = 2 * t + N_EXPERTS * tmb
    n_blocks = n_slots // tmb
    block_start = jnp.arange(n_blocks, dtype=I32) * tmb
    block_expert = jnp.minimum(
        jnp.sum((block_start[:, None] >= pad_end[None, :]).astype(I32), axis=1), N_EXPERTS - 1)
    n_used = (pad_end[-1] // tmb).astype(I32).reshape(1)
    zero_blocks = jnp.concatenate([
        jnp.maximum(pad_end - tmb, 0),
        jnp.minimum(pad_end[-1] + jnp.arange(N_EXPERTS, dtype=I32) * tmb, n_slots - tmb),
    ]).astype(I32)
    xs = _dispatch(dest1, dest2, zero_blocks, h_tiles, n_slots, tl["dispatch_tm"], tmb)
    y = _gmm(block_expert, n_used, xs, wg, wu, wd, tmb, tl["gmm_tf"])
    return _combine(dest1, dest2, x, meta_col, y, norm_final, tl["combine_tm"])


def kernel(x, norm_mix, w_in, conv_lru_w, conv_lru_b, lru_w_r, lru_b_r, lru_w_i, lru_b_i,
           lru_lambda, lru_out_norm, conv_qkv_w, dn_a_log, dn_dt_bias, dn_out_norm, w_out,
           norm_ffn, ffn_w_gate, ffn_w_up, ffn_w_down, moe_router, moe_w_gate, moe_w_up,
           moe_w_down, norm_final):
    bsz, seq, d = x.shape
    t = bsz * seq
    depth = w_in.shape[0]
    tl = _tiles(t, seq)
    row = lambda v: v.reshape(1, -1)
    xt = x.reshape(t, d)
    out = None
    moe_bf16 = {}
    w_main = w_in[:, :, :D_PROJ].astype(BF16)
    for l in range(depth):
        is_moe = l % 2 == 1
        w_small = w_in[l, :, D_PROJ:]
        w_small_p = jnp.pad(w_small, ((0, 0), (0, LANES - 2 * DN_HEADS))).astype(BF16)
        proj, ba_col, ba_row = _in_proj(xt, row(norm_mix[l]), w_main, l, w_small_p,
                                        w_small.T.astype(BF16), tl["proj_tm"], tl["proj_tn"])
        y_lru = _rg_lru(proj, conv_lru_w[l], row(conv_lru_b[l]), lru_w_r[l].astype(BF16),
                        row(lru_b_r[l]), lru_w_i[l].astype(BF16), row(lru_b_i[l]),
                        row(lru_lambda[l]), row(lru_out_norm[l]), bsz, tl["lru_tm"])
        y_dn = _gated_deltanet(proj, ba_col, ba_row, conv_qkv_w[l], dn_a_log[l], dn_dt_bias[l],
                               row(dn_out_norm[l]), bsz, tl["gdn_tm"], tl["gdn_hb"])
        xt, h2 = _out_proj(y_lru, y_dn, w_out[l].astype(BF16), xt, row(norm_ffn[l]),
                           tl["out_tm"], rows_as_tiles=is_moe)
        j = l // 2
        if not is_moe:
            nxt = moe_w_gate[j], moe_w_up[j], moe_w_down[j]
            steps = (t // tl["ffn_tm"]) * (ffn_w_gate.shape[2] // tl["ffn_tf"])
            flat = tuple(w.reshape(-1, w.shape[-1]) for w in nxt)
            ride = l + 1 < depth and all(
                a.shape[0] % CAST_ROWS == 0 and a.shape[0] // CAST_ROWS <= steps for a in flat)
            xt, cast = _ffn_dense(h2, xt, ffn_w_gate[j].astype(BF16), ffn_w_up[j].astype(BF16),
                                  ffn_w_down[j].astype(BF16), tl["ffn_tm"], tl["ffn_tf"],
                                  flat if ride else ())
            if ride:
                moe_bf16[j] = tuple(c.reshape(w.shape) for c, w in zip(cast, nxt))
        else:
            assert l == depth - 1, "the routed layer fuses the final RMSNorm"
            wg, wu, wd = moe_bf16.get(j) or (
                moe_w_gate[j].astype(BF16), moe_w_up[j].astype(BF16), moe_w_down[j].astype(BF16))
            out = _moe(xt, h2, row(norm_ffn[l]), moe_router[j], wg, wu, wd, row(norm_final), tl)
    return out.reshape(bsz, seq, d)
```

```python
import functools

import jax
import jax.numpy as jnp
from jax import lax
from jax.experimental import pallas as pl
from jax.experimental.pallas import tpu as pltpu

F32 = jnp.float32
BF16 = jnp.bfloat16
I32 = jnp.int32

D_MODEL = 2048
D_LRU = 1024
LRU_BLOCKS = 8
LRU_BLOCK_W = 128
LRU_C = 8.0
CONV_W = 4
D_DN = 1024
DN_HEAD_DIM = 128
DN_HEADS = 8
DN_CHUNK = 64
D_PROJ = 2 * D_LRU + 4 * D_DN
N_EXPERTS = 8
EPS = 1e-6

LANES = 128
SUBLANES = 8
ROW_CHUNKS = D_MODEL // LANES
VMEM_LIMIT = 56 * 1024 * 1024


def _params(*sem):
    return pltpu.CompilerParams(dimension_semantics=sem, vmem_limit_bytes=VMEM_LIMIT)


def _rms(x, gain):
    ms = jnp.mean(x * x, axis=-1, keepdims=True)
    return x * lax.rsqrt(ms + EPS) * gain


def _softplus(x):
    return jnp.maximum(x, 0.0) + jnp.log1p(jnp.exp(-jnp.abs(x)))


def _dot(a, b):
    return jnp.dot(a, b, preferred_element_type=F32)


def _dot_nt(a, b, precision=None):
    return lax.dot_general(a, b, (((1,), (1,)), ((), ())),
                           preferred_element_type=F32, precision=precision)


def _dot_tn(a, b):
    return lax.dot_general(a, b, (((0,), (0,)), ((), ())), preferred_element_type=F32)


def _inproj_kernel(x_ref, g_ref, w_ref, ws_ref, wst_ref, o_ref, oc_ref, or_ref, h_ref):
    @pl.when(pl.program_id(1) == 0)
    def _():
        h = _rms(x_ref[...], g_ref[...]).astype(BF16)
        h_ref[...] = h
        oc_ref[...] = _dot(h, ws_ref[...])
        or_ref[...] = _dot_nt(wst_ref[...], h)

    o_ref[...] = _dot(h_ref[...], w_ref[...])


def _in_proj(x, gain, w_main, layer, w_small, w_small_t, tm, tn):
    t, d = x.shape
    n = w_main.shape[2]
    return pl.pallas_call(
        _inproj_kernel,
        grid=(t // tm, n // tn),
        in_specs=[
            pl.BlockSpec((tm, d), lambda i, j: (i, 0)),
            pl.BlockSpec((1, d), lambda i, j: (0, 0)),
            pl.BlockSpec((None, d, tn), lambda i, j: (layer, 0, j)),
            pl.BlockSpec((d, LANES), lambda i, j: (0, 0)),
            pl.BlockSpec((2 * DN_HEADS, d), lambda i, j: (0, 0)),
        ],
        out_specs=[
            pl.BlockSpec((tm, tn), lambda i, j: (i, j)),
            pl.BlockSpec((tm, LANES), lambda i, j: (i, 0)),
            pl.BlockSpec((2 * DN_HEADS, tm), lambda i, j: (0, i)),
        ],
        out_shape=[
            jax.ShapeDtypeStruct((t, n), F32),
            jax.ShapeDtypeStruct((t, LANES), F32),
            jax.ShapeDtypeStruct((2 * DN_HEADS, t), F32),
        ],
        scratch_shapes=[pltpu.VMEM((tm, d), BF16)],
        compiler_params=_params("parallel", "arbitrary"),
        name="in_proj",
    )(x, gain, w_main, w_small, w_small_t)


def _lru_kernel(x_ref, gate_ref, cw_ref, cb_ref, wr_ref, br_ref, wi_ref, bi_ref, lam_ref,
                on_ref, o_ref, xs_ref, hc_ref, y_ref, *, tm):
    @pl.when(pl.program_id(1) == 0)
    def _():
        xs_ref[0:SUBLANES, :] = jnp.zeros((SUBLANES, D_LRU), F32)
        hc_ref[...] = jnp.zeros_like(hc_ref)

    xs_ref[SUBLANES:SUBLANES + tm, :] = x_ref[...]
    row = lax.broadcasted_iota(I32, (tm, LRU_BLOCK_W), 0)
    ssq = jnp.zeros((tm, 1), F32)
    for n in range(LRU_BLOCKS):
        sl = slice(n * LRU_BLOCK_W, (n + 1) * LRU_BLOCK_W)
        x_ext = xs_ref[:, sl]
        xc = pltpu.roll(x_ext, CONV_W - 1, 0)[SUBLANES:, :] * cw_ref[0:1, sl]
        for k in range(1, CONV_W - 1):
            xc = xc + pltpu.roll(x_ext, CONV_W - 1 - k, 0)[SUBLANES:, :] * cw_ref[k:k + 1, sl]
        xc = xc + x_ext[SUBLANES:, :] * cw_ref[CONV_W - 1:CONV_W, sl]
        xc = xc + cb_ref[:, sl]
        xb = xc.astype(BF16)
        r = jax.nn.sigmoid(_dot(xb, wr_ref[n]) + br_ref[:, sl])
        i = jax.nn.sigmoid(_dot(xb, wi_ref[n]) + bi_ref[:, sl])
        log_a = -LRU_C * r * _softplus(-lam_ref[:, sl])
        a = jnp.exp(log_a)
        b = jnp.sqrt(-jnp.tanh(log_a) * (1.0 + a * a)) * (i * xc)
        d = 1
        while d < tm:
            keep = row >= d
            b = jnp.where(keep, a * pltpu.roll(b, d, 0) + b, b)
            a = jnp.where(keep, a * pltpu.roll(a, d, 0), a)
            d *= 2
        h = b + a * hc_ref[0:1, sl]
        hc_ref[0:1, sl] = h[tm - 1:tm, :]
        y = h * jax.nn.gelu(gate_ref[:, sl], approximate=True)
        y_ref[:, sl] = y
        ssq = ssq + jnp.sum(y * y, axis=-1, keepdims=True)
    inv = lax.rsqrt(ssq / D_LRU + EPS)
    o_ref[...] = (y_ref[...] * inv * on_ref[...]).astype(BF16)
    xs_ref[0:SUBLANES, :] = x_ref[tm - SUBLANES:tm, :]


def _rg_lru(proj, conv_w, conv_b, w_r, b_r, w_i, b_i, lam, out_norm, bsz, tm):
    t = proj.shape[0]
    ns = t // bsz // tm
    row = lambda b, s: (0, 0)
    return pl.pallas_call(
        functools.partial(_lru_kernel, tm=tm),
        grid=(bsz, ns),
        in_specs=[
            pl.BlockSpec((tm, D_LRU), lambda b, s: (b * ns + s, 0)),
            pl.BlockSpec((tm, D_LRU), lambda b, s: (b * ns + s, 1)),
            pl.BlockSpec((CONV_W, D_LRU), row),
            pl.BlockSpec((1, D_LRU), row),
            pl.BlockSpec((LRU_BLOCKS, LRU_BLOCK_W, LRU_BLOCK_W), lambda b, s: (0, 0, 0)),
            pl.BlockSpec((1, D_LRU), row),
            pl.BlockSpec((LRU_BLOCKS, LRU_BLOCK_W, LRU_BLOCK_W), lambda b, s: (0, 0, 0)),
            pl.BlockSpec((1, D_LRU), row),
            pl.BlockSpec((1, D_LRU), row),
            pl.BlockSpec((1, D_LRU), row),
        ],
        out_specs=pl.BlockSpec((tm, D_LRU), lambda b, s: (b * ns + s, 0)),
        out_shape=jax.ShapeDtypeStruct((t, D_LRU), BF16),
        scratch_shapes=[
            pltpu.VMEM((tm + SUBLANES, D_LRU), F32),
            pltpu.VMEM((SUBLANES, D_LRU), F32),
            pltpu.VMEM((tm, D_LRU), F32),
        ],
        compiler_params=_params("parallel", "arbitrary"),
        name="rg_lru",
    )(proj, proj, conv_w, conv_b, w_r, b_r, w_i, b_i, lam, out_norm)


def _gdn_kernel(q_ref, k_ref, v_ref, gate_ref, cwq_ref, cwk_ref, cwv_ref, bac_ref, bar_ref,
                alc_ref, dtc_ref, alr_ref, dtr_ref, on_ref, o_ref,
                qs_ref, ks_ref, vs_ref, st_ref, gr_ref, *, tm, hb):
    hg = pl.program_id(1)
    c = DN_CHUNK

    @pl.when(pl.program_id(2) == 0)
    def _():
        zeros = jnp.zeros((SUBLANES, hb * DN_HEAD_DIM), F32)
        qs_ref[0:SUBLANES, :] = zeros
        ks_ref[0:SUBLANES, :] = zeros
        vs_ref[0:SUBLANES, :] = zeros
        st_ref[...] = jnp.zeros_like(st_ref)

    def conv_silu(x_ref, xs_ref, cw_ref):
        xs_ref[SUBLANES:SUBLANES + tm, :] = x_ref[...]
        x_ext = xs_ref[...]
        y = pltpu.roll(x_ext, CONV_W - 1, 0)[SUBLANES:, :] * cw_ref[0:1, :]
        for kk in range(1, CONV_W - 1):
            y = y + pltpu.roll(x_ext, CONV_W - 1 - kk, 0)[SUBLANES:, :] * cw_ref[kk:kk + 1, :]
        y = y + x_ext[SUBLANES:, :] * cw_ref[CONV_W - 1:CONV_W, :]
        xs_ref[0:SUBLANES, :] = x_ref[tm - SUBLANES:tm, :]
        return jax.nn.silu(y)

    q_all = conv_silu(q_ref, qs_ref, cwq_ref)
    k_all = conv_silu(k_ref, ks_ref, cwk_ref)
    v_all = conv_silu(v_ref, vs_ref, cwv_ref)

    bac = bac_ref[...]
    lane = lax.broadcasted_iota(I32, (tm, LANES), 1)
    rowc = lax.broadcasted_iota(I32, (tm, LANES), 0) % c
    g_all = -jnp.exp(alc_ref[...]) * _softplus(bac + dtc_ref[...])
    d = 1
    while d < c:
        g_all = g_all + jnp.where(rowc >= d, pltpu.roll(g_all, d, 0), 0.0)
        d *= 2
    sig_all = jax.nn.sigmoid(bac)

    g_row = -jnp.exp(alr_ref[...]) * _softplus(bar_ref[DN_HEADS:2 * DN_HEADS, :] + dtr_ref[...])
    lanec = lax.broadcasted_iota(I32, (DN_HEADS, tm), 1) % c
    d = 1
    while d < c:
        g_row = g_row + jnp.where(lanec >= d, pltpu.roll(g_row, d, 1), 0.0)
        d *= 2
    gr_ref[...] = g_row

    ri = lax.broadcasted_iota(I32, (c, c), 0)
    ci = lax.broadcasted_iota(I32, (c, c), 1)
    causal = ri >= ci
    strict = ri > ci
    eye = (ri == ci).astype(F32)
    on = on_ref[...]
    n_ch = tm // c

    probs = []
    for j in range(hb):
        hd = hg * hb + j
        cs = slice(j * DN_HEAD_DIM, (j + 1) * DN_HEAD_DIM)
        q = q_all[:, cs]
        k = k_all[:, cs]
        q = q * lax.rsqrt(jnp.sum(q * q, axis=-1, keepdims=True) + EPS) * (DN_HEAD_DIM ** -0.5)
        k = k * lax.rsqrt(jnp.sum(k * k, axis=-1, keepdims=True) + EPS)
        gcum = jnp.sum(jnp.where(lane == hd + DN_HEADS, g_all, 0.0), axis=1, keepdims=True)
        beta = jnp.sum(jnp.where(lane == hd, sig_all, 0.0), axis=1, keepdims=True)
        grow_all = gr_ref[pl.ds(hd, 1), :]
        for ch in range(n_ch):
            rs = slice(ch * c, (ch + 1) * c)
            gcol = gcum[rs]
            kc = k[rs]
            kb = kc * beta[rs]
            probs.append(dict(
                j=j, ch=ch, qc=q[rs], kc=kc, kb=kb, vb=v_all[rs, cs] * beta[rs], gcol=gcol,
                eg=jnp.exp(gcol), glast=gcol[c - 1:c, :],
                decay=jnp.where(causal, jnp.exp(jnp.where(causal, gcol - grow_all[:, rs], 0.0)), 0.0)))
    for p in probs:
        p["prod"] = _dot_nt(jnp.concatenate([p["qc"], p["kb"]], axis=0).astype(BF16),
                            p["kc"].astype(BF16))
    for p in probs:
        p["qk"] = jnp.where(causal, p["prod"][0:c] * p["decay"], 0.0).astype(BF16)
        low = jnp.where(strict, p["prod"][c:2 * c] * p["decay"], 0.0)
        p["inv"] = eye - low
        p["pw"] = low.astype(BF16)
    n_sq = 1
    while 2 * n_sq < c:
        for p in probs:
            p["pw"] = _dot(p["pw"], p["pw"]).astype(BF16)
        for p in probs:
            p["inv"] = p["inv"] + _dot(p["inv"].astype(BF16), p["pw"])
        n_sq *= 2
    for p in probs:
        rhs = jnp.concatenate([p["vb"], p["kb"] * p["eg"]], axis=1).astype(BF16)
        sol = _dot(p["inv"].astype(BF16), rhs)
        p["u"] = sol[:, 0:DN_HEAD_DIM]
        w = sol[:, DN_HEAD_DIM:2 * DN_HEAD_DIM]
        p["wq"] = jnp.concatenate([w, p["qc"] * p["eg"]], axis=0).astype(BF16)
        p["k_dec"] = (p["kc"] * jnp.exp(p["glast"] - p["gcol"])).astype(BF16)
        p["g_end"] = jnp.exp(p["glast"])

    states = [st_ref[j] for j in range(hb)]
    for ch in range(n_ch):
        rs = slice(ch * c, (ch + 1) * c)
        cur = [probs[j * n_ch + ch] for j in range(hb)]
        wss = [_dot(p["wq"], states[j].astype(BF16)) for j, p in enumerate(cur)]
        v_nbs = [(p["u"] - ws[0:c]).astype(BF16) for p, ws in zip(cur, wss)]
        outs = [ws[c:2 * c] + _dot(p["qk"], v_nb) for p, ws, v_nb in zip(cur, wss, v_nbs)]
        states = [states[j] * p["g_end"] + _dot_tn(p["k_dec"], v_nb)
                  for j, (p, v_nb) in enumerate(zip(cur, v_nbs))]
        for j, o in enumerate(outs):
            cs = slice(j * DN_HEAD_DIM, (j + 1) * DN_HEAD_DIM)
            o_ref[rs, cs] = (_rms(o, on) * jax.nn.silu(gate_ref[rs, cs])).astype(BF16)
    for j in range(hb):
        st_ref[j] = states[j]


def _gated_deltanet(proj, ba_col, ba_row, conv_w, a_log, dt_bias, out_norm, bsz, tm, hb):
    t = proj.shape[0]
    ns = t // bsz // tm
    hh = DN_HEADS
    ng = hh // hb
    wb = hb * DN_HEAD_DIM
    q0 = 2 * D_LRU // wb
    pad = jnp.zeros((hh,), F32)
    al_c = jnp.concatenate([pad, a_log, jnp.zeros((LANES - 2 * hh,), F32)]).reshape(1, LANES)
    dt_c = jnp.concatenate([pad, dt_bias, jnp.zeros((LANES - 2 * hh,), F32)]).reshape(1, LANES)
    al_r = a_log.reshape(hh, 1)
    dt_r = dt_bias.reshape(hh, 1)
    tok = lambda col0: (lambda b, h, s: (b * ns + s, col0 + h))
    cw = lambda col0: (lambda b, h, s: (0, col0 + h))
    const = lambda b, h, s: (0, 0)
    return pl.pallas_call(
        functools.partial(_gdn_kernel, tm=tm, hb=hb),
        grid=(bsz, ng, ns),
        in_specs=[
            pl.BlockSpec((tm, wb), tok(q0)),
            pl.BlockSpec((tm, wb), tok(q0 + ng)),
            pl.BlockSpec((tm, wb), tok(q0 + 2 * ng)),
            pl.BlockSpec((tm, wb), tok(q0 + 3 * ng)),
            pl.BlockSpec((CONV_W, wb), cw(0)),
            pl.BlockSpec((CONV_W, wb), cw(ng)),
            pl.BlockSpec((CONV_W, wb), cw(2 * ng)),
            pl.BlockSpec((tm, LANES), lambda b, h, s: (b * ns + s, 0)),
            pl.BlockSpec((2 * hh, tm), lambda b, h, s: (0, b * ns + s)),
            pl.BlockSpec((1, LANES), const),
            pl.BlockSpec((1, LANES), const),
            pl.BlockSpec((hh, 1), const),
            pl.BlockSpec((hh, 1), const),
            pl.BlockSpec((1, DN_HEAD_DIM), const),
        ],
        out_specs=pl.BlockSpec((tm, wb), lambda b, h, s: (b * ns + s, h)),
        out_shape=jax.ShapeDtypeStruct((t, D_DN), BF16),
        scratch_shapes=[
            pltpu.VMEM((tm + SUBLANES, wb), F32),
            pltpu.VMEM((tm + SUBLANES, wb), F32),
            pltpu.VMEM((tm + SUBLANES, wb), F32),
            pltpu.VMEM((hb, DN_HEAD_DIM, DN_HEAD_DIM), F32),
            pltpu.VMEM((hh, tm), F32),
        ],
        compiler_params=_params("parallel", "parallel", "arbitrary"),
        name="gated_deltanet",
    )(proj, proj, proj, proj, conv_w, conv_w, conv_w, ba_col, ba_row,
      al_c, dt_c, al_r, dt_r, out_norm)


def _outproj_kernel(yl_ref, yd_ref, w_ref, x_ref, g_ref, xo_ref, h_ref, *, rows_as_tiles):
    acc = _dot(yl_ref[...], w_ref[0:D_LRU, :]) + _dot(yd_ref[...], w_ref[D_LRU:D_LRU + D_DN, :])
    xn = x_ref[...] + acc
    xo_ref[...] = xn
    hn = _rms(xn, g_ref[...])
    if rows_as_tiles:
        for cc in range(ROW_CHUNKS):
            h_ref[:, cc, :] = hn[:, cc * LANES:(cc + 1) * LANES]
    else:
        h_ref[...] = hn.astype(h_ref.dtype)


def _out_proj(y_lru, y_dn, w, x, gain, tm, rows_as_tiles):
    t, d = x.shape
    if rows_as_tiles:
        h_spec = pl.BlockSpec((tm, ROW_CHUNKS, LANES), lambda i: (i, 0, 0))
        h_shape = jax.ShapeDtypeStruct((t, ROW_CHUNKS, LANES), F32)
    else:
        h_spec = pl.BlockSpec((tm, d), lambda i: (i, 0))
        h_shape = jax.ShapeDtypeStruct((t, d), BF16)
    return pl.pallas_call(
        functools.partial(_outproj_kernel, rows_as_tiles=rows_as_tiles),
        grid=(t // tm,),
        in_specs=[
            pl.BlockSpec((tm, D_LRU), lambda i: (i, 0)),
            pl.BlockSpec((tm, D_DN), lambda i: (i, 0)),
            pl.BlockSpec((D_LRU + D_DN, d), lambda i: (0, 0)),
            pl.BlockSpec((tm, d), lambda i: (i, 0)),
            pl.BlockSpec((1, d), lambda i: (0, 0)),
        ],
        out_specs=[pl.BlockSpec((tm, d), lambda i: (i, 0)), h_spec],
        out_shape=[jax.ShapeDtypeStruct((t, d), F32), h_shape],
        compiler_params=_params("parallel"),
        name="out_proj",
    )(y_lru, y_dn, w, x, gain)


CAST_ROWS = 64


def _ffn_kernel(h_ref, x_ref, wg_ref, wu_ref, wd_ref, *rest, cast_blocks):
    n_cast = len(cast_blocks)
    src_refs, o_ref, dst_refs = rest[:n_cast], rest[n_cast], rest[n_cast + 1:]

    @pl.when(pl.program_id(1) == 0)
    def _():
        o_ref[...] = x_ref[...]

    h = h_ref[...]
    g = _dot(h, wg_ref[...])
    u = _dot(h, wu_ref[...])
    o_ref[...] += _dot((jax.nn.silu(g) * u).astype(BF16), wd_ref[...])

    step = pl.program_id(0) * pl.num_programs(1) + pl.program_id(1)
    for src, dst, nb in zip(src_refs, dst_refs, cast_blocks):
        @pl.when(step < nb)
        def _(src=src, dst=dst):
            dst[...] = src[...].astype(BF16)


def _ffn_dense(h, x, wg, wu, wd, tm, tf, to_cast=()):
    t, d = x.shape
    ff = wg.shape[1]
    nf = ff // tf
    cast_blocks = tuple(a.shape[0] // CAST_ROWS for a in to_cast)
    assert all(nb <= (t // tm) * nf for nb in cast_blocks)
    cast_spec = lambda a, nb: pl.BlockSpec(
        (CAST_ROWS, a.shape[1]), lambda i, f: (jnp.minimum(i * nf + f, nb - 1), 0))
    cast_specs = [cast_spec(a, nb) for a, nb in zip(to_cast, cast_blocks)]
    res = pl.pallas_call(
        functools.partial(_ffn_kernel, cast_blocks=cast_blocks),
        grid=(t // tm, nf),
        in_specs=[
            pl.BlockSpec((tm, d), lambda i, f: (i, 0)),
            pl.BlockSpec((tm, d), lambda i, f: (i, 0)),
            pl.BlockSpec((d, tf), lambda i, f: (0, f)),
            pl.BlockSpec((d, tf), lambda i, f: (0, f)),
            pl.BlockSpec((tf, d), lambda i, f: (f, 0)),
        ] + cast_specs,
        out_specs=[pl.BlockSpec((tm, d), lambda i, f: (i, 0))] + cast_specs,
        out_shape=[jax.ShapeDtypeStruct((t, d), F32)]
        + [jax.ShapeDtypeStruct(a.shape, BF16) for a in to_cast],
        compiler_params=_params("arbitrary", "arbitrary"),
        name="ffn_dense",
    )(h, x, wg, wu, wd, *to_cast)
    return res[0], tuple(res[1:])


R_E1, R_E2, R_W1, R_W2, R_RANK1, R_RANK2 = range(6)


def _router_kernel(x_ref, g_ref, rt_ref, mr_ref, mc_ref, cnt_ref, carry_ref, *, tm):
    @pl.when(pl.program_id(0) == 0)
    def _():
        carry_ref[...] = jnp.zeros_like(carry_ref)

    h = _rms(x_ref[...], g_ref[...])
    rt = rt_ref[...]
    h_hi = h.astype(BF16)
    h_lo = (h - h_hi.astype(F32)).astype(BF16)
    rt_hi = rt.astype(BF16)
    rt_lo = (rt - rt_hi.astype(F32)).astype(BF16)
    logits = _dot_nt(rt_hi, h_hi) + (_dot_nt(rt_hi, h_lo) + _dot_nt(rt_lo, h_hi))
    eidx = lax.broadcasted_iota(I32, (N_EXPERTS, tm), 0)
    m1 = jnp.max(logits, axis=0, keepdims=True)
    i1 = jnp.min(jnp.where(logits == m1, eidx, N_EXPERTS), axis=0, keepdims=True)
    rest = jnp.where(eidx == i1, -jnp.inf, logits)
    m2 = jnp.max(rest, axis=0, keepdims=True)
    i2 = jnp.min(jnp.where(rest == m2, eidx, N_EXPERTS), axis=0, keepdims=True)
    e2 = jnp.exp(m2 - m1)
    w1 = 1.0 / (1.0 + e2)
    w2 = e2 / (1.0 + e2)
    oh1 = eidx == i1
    oh2 = eidx == i2
    cnt = jnp.where(oh1 | oh2, 1.0, 0.0)
    ti = lax.broadcasted_iota(I32, (tm, tm), 0)
    tj = lax.broadcasted_iota(I32, (tm, tm), 1)
    upper = jnp.where(ti <= tj, 1.0, 0.0).astype(BF16)
    cum = _dot(cnt.astype(BF16), upper)
    before = cum - cnt + carry_ref[:, 0:1]
    rank1 = jnp.sum(jnp.where(oh1, before, 0.0), axis=0, keepdims=True)
    rank2 = jnp.sum(jnp.where(oh2, before, 0.0), axis=0, keepdims=True)
    total = carry_ref[:, 0:1] + cum[:, tm - 1:tm]
    carry_ref[...] = jnp.broadcast_to(total, carry_ref.shape)
    cnt_ref[...] = jnp.broadcast_to(total, cnt_ref.shape).astype(I32)
    rec = jnp.concatenate(
        [i1.astype(F32), i2.astype(F32), w1, w2, rank1, rank2,
         jnp.zeros((LANES - 6, tm), F32)], axis=0)
    mr_ref[...] = rec[0:SUBLANES, :]
    mc_ref[...] = rec.T


def _router(x, gain, router_t, tm):
    t, d = x.shape
    return pl.pallas_call(
        functools.partial(_router_kernel, tm=tm),
        grid=(t // tm,),
        in_specs=[
            pl.BlockSpec((tm, d), lambda i: (i, 0)),
            pl.BlockSpec((1, d), lambda i: (0, 0)),
            pl.BlockSpec((N_EXPERTS, d), lambda i: (0, 0)),
        ],
        out_specs=[
            pl.BlockSpec((SUBLANES, tm), lambda i: (0, i)),
            pl.BlockSpec((tm, LANES), lambda i: (i, 0)),
            pl.BlockSpec((N_EXPERTS, LANES), lambda i: (0, 0)),
        ],
        out_shape=[
            jax.ShapeDtypeStruct((SUBLANES, t), F32),
            jax.ShapeDtypeStruct((t, LANES), F32),
            jax.ShapeDtypeStruct((N_EXPERTS, LANES), I32),
        ],
        scratch_shapes=[pltpu.VMEM((N_EXPERTS, LANES), F32)],
        compiler_params=_params("arbitrary"),
        name="moe_router",
    )(x, gain, router_t)


DMA_UNROLL = 8


def _dispatch_kernel(d1_ref, d2_ref, zb_ref, h_ref, xs_ref, z_ref, sem, zsem, *, tm, tmb, n_zero):
    base = pl.program_id(0) * tm

    @pl.when(pl.program_id(0) == 0)
    def _():
        z_ref[...] = jnp.zeros_like(z_ref)
        for e in range(n_zero):
            fill = pltpu.make_async_copy(z_ref, xs_ref.at[pl.ds(zb_ref[e], tmb)], zsem)
            fill.start()
            fill.wait()

    def issue(g, carry):
        for u in range(DMA_UNROLL):
            r = g * DMA_UNROLL + u
            pltpu.make_async_copy(h_ref.at[r], xs_ref.at[d1_ref[base + r]], sem.at[0]).start()
            pltpu.make_async_copy(h_ref.at[r], xs_ref.at[d2_ref[base + r]], sem.at[1]).start()
        return carry

    lax.fori_loop(0, tm // DMA_UNROLL, issue, 0)
    pltpu.make_async_copy(h_ref, xs_ref.at[pl.ds(0, tm)], sem.at[0]).wait()
    pltpu.make_async_copy(h_ref, xs_ref.at[pl.ds(0, tm)], sem.at[1]).wait()


def _dispatch(dest1, dest2, zero_blocks, h_tiles, n_slots, tm, tmb):
    t = h_tiles.shape[0]
    return pl.pallas_call(
        functools.partial(_dispatch_kernel, tm=tm, tmb=tmb, n_zero=zero_blocks.shape[0]),
        grid_spec=pltpu.PrefetchScalarGridSpec(
            num_scalar_prefetch=3,
            grid=(t // tm,),
            in_specs=[pl.BlockSpec((tm, ROW_CHUNKS, LANES), lambda i, d1, d2, zb: (i, 0, 0))],
            out_specs=pl.BlockSpec(memory_space=pl.ANY),
            scratch_shapes=[
                pltpu.VMEM((tmb, ROW_CHUNKS, LANES), F32),
                pltpu.SemaphoreType.DMA((2,)),
                pltpu.SemaphoreType.DMA(()),
            ],
        ),
        out_shape=jax.ShapeDtypeStruct((n_slots, ROW_CHUNKS, LANES), F32),
        compiler_params=_params("arbitrary"),
        name="moe_dispatch",
    )(dest1, dest2, zero_blocks, h_tiles)


def _gmm_kernel(be_ref, nu_ref, xs_ref, wg_ref, wu_ref, wd_ref, y_ref, xb_ref, acc_ref):
    del be_ref
    f = pl.program_id(1)
    used = pl.program_id(0) < nu_ref[0]

    @pl.when(jnp.logical_and(jnp.logical_not(used), f == pl.num_programs(1) - 1))
    def _():
        y_ref[...] = jnp.zeros_like(y_ref)

    @pl.when(used)
    def _():
        @pl.when(f == 0)
        def _():
            xb_ref[...] = jnp.concatenate(
                [xs_ref[:, cc, :] for cc in range(ROW_CHUNKS)], axis=1).astype(BF16)
            acc_ref[...] = jnp.zeros_like(acc_ref)

        xb = xb_ref[...]
        g = _dot(xb, wg_ref[...])
        u = _dot(xb, wu_ref[...])
        acc_ref[...] += _dot((jax.nn.silu(g) * u).astype(BF16), wd_ref[...])

        @pl.when(f == pl.num_programs(1) - 1)
        def _():
            for cc in range(ROW_CHUNKS):
                y_ref[:, cc, :] = acc_ref[:, cc * LANES:(cc + 1) * LANES]


def _gmm(block_expert, n_used, xs, wg, wu, wd, tm, tf):
    n_slots = xs.shape[0]
    d = D_MODEL
    ff = wg.shape[2]
    nf = ff // tf

    def blk(b, nu):
        return jnp.minimum(b, nu[0] - 1)

    def fidx(b, f, nu):
        return jnp.where(b < nu[0], f, nf - 1)

    return pl.pallas_call(
        _gmm_kernel,
        grid_spec=pltpu.PrefetchScalarGridSpec(
            num_scalar_prefetch=2,
            grid=(n_slots // tm, nf),
            in_specs=[
                pl.BlockSpec((tm, ROW_CHUNKS, LANES), lambda b, f, be, nu: (blk(b, nu), 0, 0)),
                pl.BlockSpec((None, d, tf), lambda b, f, be, nu: (be[blk(b, nu)], 0, fidx(b, f, nu))),
                pl.BlockSpec((None, d, tf), lambda b, f, be, nu: (be[blk(b, nu)], 0, fidx(b, f, nu))),
                pl.BlockSpec((None, tf, d), lambda b, f, be, nu: (be[blk(b, nu)], fidx(b, f, nu), 0)),
            ],
            out_specs=pl.BlockSpec((tm, ROW_CHUNKS, LANES), lambda b, f, be, nu: (b, 0, 0)),
            scratch_shapes=[pltpu.VMEM((tm, d), BF16), pltpu.VMEM((tm, d), F32)],
        ),
        out_shape=jax.ShapeDtypeStruct((n_slots, ROW_CHUNKS, LANES), F32),
        compiler_params=_params("arbitrary", "arbitrary"),
        name="moe_gmm",
    )(block_expert, n_used, xs, wg, wu, wd)


def _combine_kernel(d1_ref, d2_ref, x_ref, mc_ref, y_ref, g_ref, o_ref, b1_ref, b2_ref, sem, *, tm):
    i = pl.program_id(0)
    slot = i % 2

    def gather(step, sl):
        base = step * tm

        def issue(g, carry):
            for u in range(DMA_UNROLL):
                r = g * DMA_UNROLL + u
                pltpu.make_async_copy(y_ref.at[d1_ref[base + r]], b1_ref.at[sl, r], sem.at[sl, 0]).start()
                pltpu.make_async_copy(y_ref.at[d2_ref[base + r]], b2_ref.at[sl, r], sem.at[sl, 1]).start()
            return carry

        lax.fori_loop(0, tm // DMA_UNROLL, issue, 0)

    @pl.when(i == 0)
    def _():
        gather(0, 0)

    @pl.when(i + 1 < pl.num_programs(0))
    def _():
        gather(i + 1, 1 - slot)

    pltpu.make_async_copy(y_ref.at[pl.ds(0, tm)], b1_ref.at[slot], sem.at[slot, 0]).wait()
    pltpu.make_async_copy(y_ref.at[pl.ds(0, tm)], b2_ref.at[slot], sem.at[slot, 1]).wait()
    w1 = mc_ref[:, R_W1:R_W1 + 1]
    w2 = mc_ref[:, R_W2:R_W2 + 1]
    xn = jnp.concatenate(
        [x_ref[:, cc * LANES:(cc + 1) * LANES]
         + (w1 * b1_ref[slot, :, cc, :] + w2 * b2_ref[slot, :, cc, :])
         for cc in range(ROW_CHUNKS)], axis=1)
    o_ref[...] = _rms(xn, g_ref[...])


def _combine(dest1, dest2, x, meta_col, y, gain, tm):
    t, d = x.shape
    return pl.pallas_call(
        functools.partial(_combine_kernel, tm=tm),
        grid_spec=pltpu.PrefetchScalarGridSpec(
            num_scalar_prefetch=2,
            grid=(t // tm,),
            in_specs=[
                pl.BlockSpec((tm, d), lambda i, d1, d2: (i, 0)),
                pl.BlockSpec((tm, LANES), lambda i, d1, d2: (i, 0)),
                pl.BlockSpec(memory_space=pl.ANY),
                pl.BlockSpec((1, d), lambda i, d1, d2: (0, 0)),
            ],
            out_specs=pl.BlockSpec((tm, d), lambda i, d1, d2: (i, 0)),
            scratch_shapes=[
                pltpu.VMEM((2, tm, ROW_CHUNKS, LANES), F32),
                pltpu.VMEM((2, tm, ROW_CHUNKS, LANES), F32),
                pltpu.SemaphoreType.DMA((2, 2)),
            ],
        ),
        out_shape=jax.ShapeDtypeStruct((t, d), F32),
        compiler_params=_params("arbitrary"),
        name="moe_combine",
    )(dest1, dest2, x, meta_col, y, gain)


def _tiles(t, seq):
    pick = lambda want, total: want if total % want == 0 else total
    return dict(
        proj_tm=pick(1024, t), proj_tn=1024,
        lru_tm=pick(256, seq),
        gdn_tm=pick(512, seq), gdn_hb=4,
        out_tm=pick(512, t),
        ffn_tm=pick(512, t), ffn_tf=512,
        router_tm=pick(512, t),
        dispatch_tm=pick(512, t),
        gmm_tm=512, gmm_tf=1024,
        combine_tm=pick(256, t),
    )


def _moe(x, h_tiles, norm_ffn, router, wg, wu, wd, norm_final, tl):
    t = x.shape[0]
    tmb = tl["gmm_tm"]
    meta_row, meta_col, counts = _router(x, norm_ffn, router.T, tl["router_tm"])
    counts = counts[:, 0]
    padded = (counts + tmb - 1) // tmb * tmb
    pad_end = jnp.cumsum(padded)
    pad_start = pad_end - padded
    e1 = meta_row[R_E1].astype(I32)
    e2 = meta_row[R_E2].astype(I32)
    experts = jnp.arange(N_EXPERTS, dtype=I32)[:, None]
    start1 = jnp.sum(jnp.where(e1[None, :] == experts, pad_start[:, None], 0), axis=0)
    start2 = jnp.sum(jnp.where(e2[None, :] == experts, pad_start[:, None], 0), axis=0)
    dest1 = start1 + meta_row[R_RANK1].astype(I32)
    dest2 = start2 + meta_row[R_RANK2].astype(I32)
    n_slots = 2 * t + N_EXPERTS * tmb
    n_blocks = n_slots // tmb
    block_start = jnp.arange(n_blocks, dtype=I32) * tmb
    block_expert = jnp.minimum(
        jnp.sum((block_start[:, None] >= pad_end[None, :]).astype(I32), axis=1), N_EXPERTS - 1)
    n_used = (pad_end[-1] // tmb).astype(I32).reshape(1)
    zero_blocks = jnp.concatenate([
        jnp.maximum(pad_end - tmb, 0),
        jnp.minimum(pad_end[-1] + jnp.arange(N_EXPERTS, dtype=I32) * tmb, n_slots - tmb),
    ]).astype(I32)
    xs = _dispatch(dest1, dest2, zero_blocks, h_tiles, n_slots, tl["dispatch_tm"], tmb)
    y = _gmm(block_expert, n_used, xs, wg, wu, wd, tmb, tl["gmm_tf"])
    return _combine(dest1, dest2, x, meta_col, y, norm_final, tl["combine_tm"])


def kernel(x, norm_mix, w_in, conv_lru_w, conv_lru_b, lru_w_r, lru_b_r, lru_w_i, lru_b_i,
           lru_lambda, lru_out_norm, conv_qkv_w, dn_a_log, dn_dt_bias, dn_out_norm, w_out,
           norm_ffn, ffn_w_gate, ffn_w_up, ffn_w_down, moe_router, moe_w_gate, moe_w_up,
           moe_w_down, norm_final):
    bsz, seq, d = x.shape
    t = bsz * seq
    depth = w_in.shape[0]
    tl = _tiles(t, seq)
    row = lambda v: v.reshape(1, -1)
    xt = x.reshape(t, d)
    out = None
    moe_bf16 = {}
    w_main = w_in[:, :, :D_PROJ].astype(BF16)
    for l in range(depth):
        is_moe = l % 2 == 1
        w_small = w_in[l, :, D_PROJ:]
        w_small_p = jnp.pad(w_small, ((0, 0), (0, LANES - 2 * DN_HEADS))).astype(BF16)
        proj, ba_col, ba_row = _in_proj(xt, row(norm_mix[l]), w_main, l, w_small_p,
                                        w_small.T.astype(BF16), tl["proj_tm"], tl["proj_tn"])
        y_lru = _rg_lru(proj, conv_lru_w[l], row(conv_lru_b[l]), lru_w_r[l].astype(BF16),
                        row(lru_b_r[l]), lru_w_i[l].astype(BF16), row(lru_b_i[l]),
                        row(lru_lambda[l]), row(lru_out_norm[l]), bsz, tl["lru_tm"])
        y_dn = _gated_deltanet(proj, ba_col, ba_row, conv_qkv_w[l], dn_a_log[l], dn_dt_bias[l],
                               row(dn_out_norm[l]), bsz, tl["gdn_tm"], tl["gdn_hb"])
        xt, h2 = _out_proj(y_lru, y_dn, w_out[l].astype(BF16), xt, row(norm_ffn[l]),
                           tl["out_tm"], rows_as_tiles=is_moe)
        j = l // 2
        if not is_moe:
            nxt = moe_w_gate[j], moe_w_up[j], moe_w_down[j]
            steps = (t // tl["ffn_tm"]) * (ffn_w_gate.shape[2] // tl["ffn_tf"])
            flat = tuple(w.reshape(-1, w.shape[-1]) for w in nxt)
            ride = l + 1 < depth and all(
                a.shape[0] % CAST_ROWS == 0 and a.shape[0] // CAST_ROWS <= steps for a in flat)
            xt, cast = _ffn_dense(h2, xt, ffn_w_gate[j].astype(BF16), ffn_w_up[j].astype(BF16),
                                  ffn_w_down[j].astype(BF16), tl["ffn_tm"], tl["ffn_tf"],
                                  flat if ride else ())
            if ride:
                moe_bf16[j] = tuple(c.reshape(w.shape) for c, w in zip(cast, nxt))
        else:
            assert l == depth - 1, "the routed layer fuses the final RMSNorm"
            wg, wu, wd = moe_bf16.get(j) or (
                moe_w_gate[j].astype(BF16), moe_w_up[j].astype(BF16), moe_w_down[j].astype(BF16))
            out = _moe(xt, h2, row(norm_ffn[l]), moe_router[j], wg, wu, wd, row(norm_final), tl)
    return out.reshape(bsz, seq, d)
```
